```python
import math
import jax, jax.numpy as jnp
from jax import lax
import numpy as np

D_MODEL = 1024
BATCH = 8
SEQ = 4096
DEPTH = 4

MIX_HALF = D_MODEL // 2
HEAD_DIM = 64
ROT_DIMS = HEAD_DIM // 4
ROPE_THETA = 500000.0
GLA_DK = 64
GLA_DV = 128
N_GLA_HEADS = MIX_HALF // GLA_DV
GLA_GATE_RANK = 16
GLA_GATE_NORMALIZER = 16.0
GLA_CHUNK = 64
N_SWA_HEADS = MIX_HALF // HEAD_DIM
N_SWA_KV_HEADS = 2
WINDOW = 128
N_DIFF_HEADS = MIX_HALF // (2 * HEAD_DIM)
Q_BLOCK = 128
HGRN_EXPAND = 128
N_HGRN_HEADS = MIX_HALF // HGRN_EXPAND
HGRN_DV = MIX_HALF // N_HGRN_HEADS
HGRN_CHUNK = 64
D_FF = 2816
CONV_WIDTH = 3
N_EVEN = (DEPTH + 1) // 2
N_ODD = DEPTH // 2
EVEN_SPLITS = (N_GLA_HEADS * GLA_DK, N_GLA_HEADS * GLA_DK, N_GLA_HEADS * GLA_DV, N_GLA_HEADS * GLA_DV,
               GLA_GATE_RANK, N_SWA_HEADS * HEAD_DIM, N_SWA_KV_HEADS * HEAD_DIM, N_SWA_KV_HEADS * HEAD_DIM)
ODD_SPLITS = (N_DIFF_HEADS * 2 * HEAD_DIM, N_DIFF_HEADS * 2 * HEAD_DIM, N_DIFF_HEADS * 2 * HEAD_DIM,
              N_HGRN_HEADS * HGRN_EXPAND, N_HGRN_HEADS * HGRN_EXPAND, N_HGRN_HEADS * HGRN_DV, N_HGRN_HEADS * HGRN_DV)

kernel_name = 'hybrid_gla_swa_diff_hgrn2_convffn_adaln'

F32 = jnp.float32


def rms_norm(x, w, eps=1e-6):
    x32 = x.astype(F32)
    y = x32 * lax.rsqrt(jnp.mean(x32 * x32, axis=-1, keepdims=True) + eps)
    return (y * w.astype(F32)).astype(x.dtype)


def split_cols(t, sizes):
    idx = np.cumsum(np.array(sizes))[:-1].tolist()
    return jnp.split(t, idx, axis=-1)


def rope_tables(positions):
    inv_freq = ROPE_THETA ** (-jnp.arange(0, ROT_DIMS, 2, dtype=F32) / ROT_DIMS)
    ang = positions.astype(F32)[..., None] * inv_freq
    return jnp.cos(ang), jnp.sin(ang)


def apply_partial_rope(x, cos, sin):
    half = ROT_DIMS // 2
    shape = cos.shape[:2] + (1,) * (x.ndim - 3) + (half,)
    cs = cos.reshape(shape).astype(x.dtype)
    sn = sin.reshape(shape).astype(x.dtype)
    x1 = x[..., :half]
    x2 = x[..., half:ROT_DIMS]
    return jnp.concatenate([x1 * cs - x2 * sn, x2 * cs + x1 * sn, x[..., ROT_DIMS:]], axis=-1)


def gated_linear_chunked(q, k, v, log_g, chunk):
    bsz, seq, nh, dk = q.shape
    dv = v.shape[-1]
    n = seq // chunk

    def to_chunks(t):
        return t.reshape(bsz, n, chunk, nh, t.shape[-1]).transpose(1, 0, 3, 2, 4)

    qc, kc, vc = to_chunks(q), to_chunks(k), to_chunks(v)
    gc = to_chunks(log_g.astype(F32))
    causal = jnp.tril(jnp.ones((chunk, chunk), dtype=bool))[:, :, None]

    def step(state, inp):
        qb, kb, vb, gb = inp
        qb, kb, vb = qb.astype(F32), kb.astype(F32), vb.astype(F32)
        bcum = lax.cumsum(gb, axis=2)
        diff = bcum[:, :, :, None, :] - bcum[:, :, None, :, :]
        decay = jnp.exp(jnp.where(causal, diff, -jnp.inf))
        scores = jnp.einsum('bhik,bhjk,bhijk->bhij', qb, kb, decay)
        intra = jnp.einsum('bhij,bhjv->bhiv', scores, vb)
        inter = jnp.einsum('bhik,bhkv->bhiv', qb * jnp.exp(bcum), state)
        total = bcum[:, :, -1, :]
        state = state * jnp.exp(total)[..., None] + jnp.einsum(
            'bhjk,bhjv->bhkv', kb * jnp.exp(total[:, :, None, :] - bcum), vb)
        return state, intra + inter

    state0 = jnp.zeros((bsz, nh, dk, dv), F32)
    _, out = lax.scan(step, state0, (qc, kc, vc, gc))
    return out.transpose(1, 0, 3, 2, 4).reshape(bsz, seq, nh, dv).astype(v.dtype)


def sliding_window_sink_attention(q, k, v, sinks):
    bsz, seq, hq, d = q.shape
    hkv = k.shape[2]
    grp = hq // hkv
    nb = seq // WINDOW
    qb = q.reshape(bsz, nb, WINDOW, hkv, grp, d)

    def with_prev(t):
        tb = t.reshape(bsz, nb, WINDOW, hkv, d)
        prev = jnp.concatenate([jnp.zeros_like(tb[:, :1]), tb[:, :-1]], axis=1)
        return jnp.concatenate([prev, tb], axis=2)

    kk, vv = with_prev(k), with_prev(v)
    s = jnp.einsum('bnqhgd,bnkhd->bnhgqk', qb, kk).astype(F32) * (d ** -0.5)
    qi = jnp.arange(WINDOW)[:, None] + WINDOW
    kj = jnp.arange(2 * WINDOW)[None, :]
    rel = qi - kj
    band = (rel >= 0) & (rel < WINDOW)
    blk = jnp.arange(nb)[:, None, None]
    mask = band[None] & ((blk > 0) | (kj >= WINDOW)[None])
    s = jnp.where(mask[None, :, None, None], s, -jnp.inf)
    sink = sinks.astype(F32).reshape(1, 1, hkv, grp, 1, 1)
    m = jnp.maximum(jnp.max(s, axis=-1, keepdims=True), sink)
    p = jnp.exp(s - m)
    p = p / (jnp.sum(p, axis=-1, keepdims=True) + jnp.exp(sink - m))
    o = jnp.einsum('bnhgqk,bnkhd->bnqhgd', p.astype(v.dtype), vv)
    return o.reshape(bsz, seq, hq, d)


def differential_attention(q, k, v, lam):
    bsz, seq, nh, _, d = q.shape
    nb = seq // Q_BLOCK
    qblocks = q.reshape(bsz, nb, Q_BLOCK, nh, 2, d).transpose(1, 0, 2, 3, 4, 5)
    kpos = jnp.arange(seq)
    scale = d ** -0.5

    def one_block(args):
        qb, n = args
        s = jnp.einsum('bqhmd,bkhmd->bhmqk', qb, k).astype(F32) * scale
        qpos = n * Q_BLOCK + jnp.arange(Q_BLOCK)
        s = jnp.where(kpos[None, :] <= qpos[:, None], s, -jnp.inf)
        p = jax.nn.softmax(s, axis=-1)
        a = p[:, :, 0] - lam * p[:, :, 1]
        return jnp.einsum('bhqk,bkhe->bqhe', a.astype(v.dtype), v)

    out = lax.map(one_block, (qblocks, jnp.arange(nb)))
    return out.transpose(1, 0, 2, 3, 4).reshape(bsz, seq, nh, v.shape[-1])


def even_mixer(h, w_in, gla_gate_w, gla_gate_b, gla_norm_w, swa_sinks, w_out, cos, sin):
    bsz, seq, _ = h.shape
    gq, gk, gv, gr, glr, sq, sk, sv = split_cols(h @ w_in, EVEN_SPLITS)
    gq = gq.reshape(bsz, seq, N_GLA_HEADS, GLA_DK) * (GLA_DK ** -0.5)
    gk = gk.reshape(bsz, seq, N_GLA_HEADS, GLA_DK)
    gv = gv.reshape(bsz, seq, N_GLA_HEADS, GLA_DV)
    log_a = jax.nn.log_sigmoid((glr @ gla_gate_w + gla_gate_b).astype(F32)) / GLA_GATE_NORMALIZER
    log_a = log_a.reshape(bsz, seq, N_GLA_HEADS, GLA_DK)
    o_gla = gated_linear_chunked(gq, gk, gv, log_a, GLA_CHUNK)
    o_gla = rms_norm(o_gla, gla_norm_w).reshape(bsz, seq, -1) * jax.nn.silu(gr)
    sq = apply_partial_rope(sq.reshape(bsz, seq, N_SWA_HEADS, HEAD_DIM), cos, sin)
    sk = apply_partial_rope(sk.reshape(bsz, seq, N_SWA_KV_HEADS, HEAD_DIM), cos, sin)
    sv = sv.reshape(bsz, seq, N_SWA_KV_HEADS, HEAD_DIM)
    o_swa = sliding_window_sink_attention(sq, sk, sv, swa_sinks).reshape(bsz, seq, -1)
    return jnp.concatenate([o_gla, o_swa], axis=-1) @ w_out


def odd_mixer(h, w_in, diff_lambda, diff_norm_w, lb, hgrn_norm_w, w_out, cos, sin, lam_init):
    bsz, seq, _ = h.shape
    dq, dk, dv, hq, hf, hi, hg = split_cols(h @ w_in, ODD_SPLITS)
    dq = apply_partial_rope(dq.reshape(bsz, seq, N_DIFF_HEADS, 2, HEAD_DIM), cos, sin)
    dk = apply_partial_rope(dk.reshape(bsz, seq, N_DIFF_HEADS, 2, HEAD_DIM), cos, sin)
    dv = dv.reshape(bsz, seq, N_DIFF_HEADS, 2 * HEAD_DIM)
    lv = diff_lambda.astype(F32)
    lam = jnp.exp(jnp.sum(lv[0] * lv[1])) - jnp.exp(jnp.sum(lv[2] * lv[3])) + lam_init
    o_diff = differential_attention(dq, dk, dv, lam)
    o_diff = rms_norm(o_diff, diff_norm_w).reshape(bsz, seq, -1) * (1.0 - lam_init)
    hq = (jax.nn.silu(hq) * (HGRN_EXPAND ** -0.5)).reshape(bsz, seq, N_HGRN_HEADS, HGRN_EXPAND)
    z = hf.astype(F32)
    log_f = jnp.logaddexp(jnp.log(lb), jnp.log1p(-lb) + jax.nn.log_sigmoid(z))
    k_in = (1.0 - lb) * jax.nn.sigmoid(-z)
    log_f = log_f.reshape(bsz, seq, N_HGRN_HEADS, HGRN_EXPAND)
    k_in = k_in.reshape(bsz, seq, N_HGRN_HEADS, HGRN_EXPAND)
    hi = hi.reshape(bsz, seq, N_HGRN_HEADS, HGRN_DV)
    o_h = gated_linear_chunked(hq, k_in, hi, log_f, HGRN_CHUNK)
    o_h = rms_norm(o_h, hgrn_norm_w).reshape(bsz, seq, -1) * jax.nn.silu(hg)
    return jnp.concatenate([o_diff, o_h], axis=-1) @ w_out


def conv_ffn(h, w_in, conv_w, conv_b, w_out):
    a, u = jnp.split(h @ w_in, 2, axis=-1)
    ap = jnp.pad(a, ((0, 0), (CONV_WIDTH - 1, 0), (0, 0)))
    a = ap[:, :-2] * conv_w[0] + ap[:, 1:-1] * conv_w[1] + ap[:, 2:] * conv_w[2] + conv_b
    return (jax.nn.silu(a) * u) @ w_out


def setup_inputs(seed: int = 0) -> dict:
    key = jax.random.key(seed)
    ks = jax.random.split(key, 32)

    def nrm(k, shape, s):
        return jax.random.normal(k, shape, F32) * s

    offsets = jax.random.randint(ks[2], (BATCH, 1), 0, 4096, dtype=jnp.int32)
    return {
        'x': nrm(ks[0], (BATCH, SEQ, D_MODEL), 1.0),
        'c': nrm(ks[1], (BATCH, D_MODEL), 1.0),
        'positions': offsets + jnp.arange(SEQ, dtype=jnp.int32)[None, :],
        'mod_w': nrm(ks[3], (DEPTH, D_MODEL, 6 * D_MODEL), 0.5 * D_MODEL ** -0.5),
        'mod_b': nrm(ks[4], (DEPTH, 6 * D_MODEL), 0.01),
        'norm_mix_w': 1.0 + nrm(ks[5], (DEPTH, D_MODEL), 0.02),
        'norm_ffn_w': 1.0 + nrm(ks[6], (DEPTH, D_MODEL), 0.02),
        'ev_w_in': nrm(ks[7], (N_EVEN, D_MODEL, sum(EVEN_SPLITS)), D_MODEL ** -0.5),
        'gla_gate_w': nrm(ks[8], (N_EVEN, GLA_GATE_RANK, N_GLA_HEADS * GLA_DK), GLA_GATE_RANK ** -0.5),
        'gla_gate_b': nrm(ks[9], (N_EVEN, N_GLA_HEADS * GLA_DK), 0.01),
        'gla_norm_w': 1.0 + nrm(ks[10], (N_EVEN, GLA_DV), 0.02),
        'swa_sinks': nrm(ks[11], (N_EVEN, N_SWA_HEADS), 1.0),
        'ev_w_out': nrm(ks[12], (N_EVEN, D_MODEL, D_MODEL), D_MODEL ** -0.5),
        'od_w_in': nrm(ks[13], (N_ODD, D_MODEL, sum(ODD_SPLITS)), D_MODEL ** -0.5),
        'diff_lambda': nrm(ks[14], (N_ODD, 4, HEAD_DIM), 0.1),
        'diff_norm_w': 1.0 + nrm(ks[15], (N_ODD, 2 * HEAD_DIM), 0.02),
        'hgrn_lb_logits': nrm(ks[16], (N_ODD, N_HGRN_HEADS * HGRN_EXPAND), 1.0),
        'hgrn_norm_w': 1.0 + nrm(ks[17], (N_ODD, HGRN_DV), 0.02),
        'od_w_out': nrm(ks[18], (N_ODD, D_MODEL, D_MODEL), D_MODEL ** -0.5),
        'ffn_w_in': nrm(ks[19], (DEPTH, D_MODEL, 2 * D_FF), D_MODEL ** -0.5),
        'ffn_conv_w': nrm(ks[20], (DEPTH, CONV_WIDTH, D_FF), CONV_WIDTH ** -0.5),
        'ffn_conv_b': nrm(ks[21], (DEPTH, D_FF), 0.01),
        'ffn_w_out': nrm(ks[22], (DEPTH, D_FF, D_MODEL), D_FF ** -0.5),
        'final_norm_w': 1.0 + nrm(ks[23], (D_MODEL,), 0.02),
    }


def reference(x, c, positions, mod_w, mod_b, norm_mix_w, norm_ffn_w, ev_w_in, gla_gate_w, gla_gate_b,
              gla_norm_w, swa_sinks, ev_w_out, od_w_in, diff_lambda, diff_norm_w, hgrn_lb_logits,
              hgrn_norm_w, od_w_out, ffn_w_in, ffn_conv_w, ffn_conv_b, ffn_w_out, final_norm_w):
    cos, sin = rope_tables(positions)
    lbs = lax.cumsum(jax.nn.softmax(hgrn_lb_logits.astype(F32), axis=0), axis=0)
    lbs = lbs - lbs[0:1]
    c_act = jax.nn.silu(c)
    for l in range(DEPTH):
        mod = (c_act @ mod_w[l] + mod_b[l])[:, None, :]
        sh1, sc1, g1, sh2, sc2, g2 = jnp.split(mod, 6, axis=-1)
        h = rms_norm(x, norm_mix_w[l]) * (1.0 + sc1) + sh1
        j = l // 2
        if l % 2 == 0:
            y = even_mixer(h, ev_w_in[j], gla_gate_w[j], gla_gate_b[j], gla_norm_w[j], swa_sinks[j],
                           ev_w_out[j], cos, sin)
        else:
            lam_init = 0.8 - 0.6 * math.exp(-0.3 * l)
            y = odd_mixer(h, od_w_in[j], diff_lambda[j], diff_norm_w[j], lbs[j], hgrn_norm_w[j],
                          od_w_out[j], cos, sin, lam_init)
        x = x + g1 * y
        h = rms_norm(x, norm_ffn_w[l]) * (1.0 + sc2) + sh2
        x = x + g2 * conv_ffn(h, ffn_w_in[l], ffn_conv_w[l], ffn_conv_b[l], ffn_w_out[l])
    return rms_norm(x, final_norm_w)
```

```python
import functools
import math

import jax
import jax.numpy as jnp
from jax import lax
from jax.experimental import pallas as pl
from jax.experimental.pallas import tpu as pltpu

F32 = jnp.float32
BF16 = jnp.bfloat16

D_MODEL = 1024
DEPTH = 4
HEAD_DIM = 64
ROT_DIMS = HEAD_DIM // 4
ROPE_THETA = 500000.0
GLA_GATE_RANK = 16
GLA_GATE_NORMALIZER = 16.0
N_SWA_HEADS = 8
N_SWA_KV_HEADS = 2
WINDOW = 128
HGRN_EXPAND = 128
D_FF = 2816
MIX_HALF = D_MODEL // 2
NORM_EPS = 1e-6

LANES = 128
CHUNK = 64
SUB = 16
EXP_CLAMP = 80.0
TM_PROJ = 512
TS_GL = 512
TQ_DIFF = 256
FF_CHUNK = 256
VMEM_LIMIT = 56 * 1024 * 1024

NT_DIMS = (((1,), (1,)), ((), ()))
TN_DIMS = (((0,), (0,)), ((), ()))


def _sigmoid(z):
    return 1.0 / (1.0 + jnp.exp(-z))


def _silu(z):
    return z * _sigmoid(z)


def _log_sigmoid(z):
    return jnp.minimum(z, 0.0) - jnp.log(1.0 + jnp.exp(-jnp.abs(z)))


def _dot(a, b):
    return jnp.dot(a, b, preferred_element_type=F32)


def _modulated_norm(x, nw, sc, sh):
    ms = jnp.mean(x * x, axis=-1, keepdims=True)
    return ((x * lax.rsqrt(ms + NORM_EPS)) * nw) * (1.0 + sc) + sh


def _head_norm(o, nw):
    ms = jnp.mean(o * o, axis=-1, keepdims=True)
    return (o * lax.rsqrt(ms + NORM_EPS)) * nw


def _rope(x, c, s1, s2):
    n = x.shape[-1] // LANES
    if n > 1:
        c, s1, s2 = (jnp.concatenate([t] * n, axis=-1) for t in (c, s1, s2))
    half = ROT_DIMS // 2
    up = pltpu.roll(x, x.shape[-1] - half, 1)
    down = pltpu.roll(x, half, 1)
    return x * c + up * s1 + down * s2


def _mod_kernel(c_ref, w_ref, b_ref, o_ref):
    ca = _silu(c_ref[...]).astype(BF16)
    o_ref[0] = _dot(ca, w_ref[0].astype(BF16)) + b_ref[0]


def _modulation(c, mod_w, mod_b):
    bsz = c.shape[0]
    tn = 1024
    n_out = mod_w.shape[-1]
    return pl.pallas_call(
        _mod_kernel,
        grid=(DEPTH, n_out // tn),
        in_specs=[
            pl.BlockSpec((bsz, D_MODEL), lambda l, j: (0, 0)),
            pl.BlockSpec((1, D_MODEL, tn), lambda l, j: (l, 0, j)),
            pl.BlockSpec((1, 1, tn), lambda l, j: (l, 0, j)),
        ],
        out_specs=pl.BlockSpec((1, bsz, tn), lambda l, j: (l, 0, j)),
        out_shape=jax.ShapeDtypeStruct((DEPTH, bsz, n_out), F32),
        name="modulation",
    )(c, mod_w, mod_b.reshape(DEPTH, 1, n_out))


def _even_in_kernel(x_ref, mod_ref, nw_ref, rc_ref, rs1_ref, rs2_ref,
                    wq_ref, wk_ref, wv_ref, wr_ref, wl_ref, gw_ref, gb_ref, wsq_ref, wskv_ref,
                    gq_ref, gk_ref, gv_ref, gr_ref, ga_ref, sq_ref, skv_ref):
    m = mod_ref[...]
    h = _modulated_norm(x_ref[...], nw_ref[...], m[:, D_MODEL:2 * D_MODEL], m[:, 0:D_MODEL]).astype(BF16)
    gq_ref[...] = _dot(h, wq_ref[...]) * (HEAD_DIM ** -0.5)
    gk_ref[...] = _dot(h, wk_ref[...])
    gv_ref[...] = _dot(h, wv_ref[...]).astype(BF16)
    gr_ref[...] = _silu(_dot(h, wr_ref[...]))
    glr = _dot(h, wl_ref[...]).astype(BF16)
    gate = _dot(glr, gw_ref[...]) + gb_ref[...]
    ga_ref[...] = _log_sigmoid(gate) / GLA_GATE_NORMALIZER
    c, s1, s2 = rc_ref[...], rs1_ref[...], rs2_ref[...]
    sq = _dot(h, wsq_ref[...])
    sq_ref[...] = (_rope(sq, c, s1, s2) * (HEAD_DIM ** -0.5)).astype(BF16)
    skv = _dot(h, wskv_ref[...])
    sk = _rope(skv[:, :LANES], c, s1, s2)
    skv_ref[...] = jnp.concatenate([sk, skv[:, LANES:]], axis=-1).astype(BF16)


def _odd_in_kernel(x_ref, mod_ref, nw_ref, rc_ref, rs1_ref, rs2_ref, llb_ref, l1m_ref, oml_ref,
                   wdq_ref, wdk_ref, wdv_ref, whq_ref, whf_ref, whi_ref, whg_ref,
                   dq_ref, dk_ref, dv_ref, hq_ref, hk_ref, hf_ref, hi_ref, hg_ref):
    m = mod_ref[...]
    h = _modulated_norm(x_ref[...], nw_ref[...], m[:, D_MODEL:2 * D_MODEL], m[:, 0:D_MODEL]).astype(BF16)
    c, s1, s2 = rc_ref[...], rs1_ref[...], rs2_ref[...]
    dq_ref[...] = (_rope(_dot(h, wdq_ref[...]), c, s1, s2) * (HEAD_DIM ** -0.5)).astype(BF16)
    dk_ref[...] = _rope(_dot(h, wdk_ref[...]), c, s1, s2).astype(BF16)
    dv_ref[...] = _dot(h, wdv_ref[...]).astype(BF16)
    hq_ref[...] = _silu(_dot(h, whq_ref[...])) * (HGRN_EXPAND ** -0.5)
    z = _dot(h, whf_ref[...])
    a1 = llb_ref[...]
    a2 = l1m_ref[...] + _log_sigmoid(z)
    hf_ref[...] = jnp.maximum(a1, a2) + jnp.log(1.0 + jnp.exp(-jnp.abs(a1 - a2)))
    hk_ref[...] = oml_ref[...] * _sigmoid(-z)
    hi_ref[...] = _dot(h, whi_ref[...]).astype(BF16)
    hg_ref[...] = _silu(_dot(h, whg_ref[...]))


def _tok_spec(width):
    return pl.BlockSpec((None, TM_PROJ, width), lambda b, t: (b, t, 0))


def _const_spec(shape):
    nd = len(shape)
    return pl.BlockSpec(shape, lambda b, t: (0,) * nd)


def _in_projection(kern, name, x, mod_l, nw, tables, consts, weights, outs):
    bsz, seq, _ = x.shape
    in_specs = ([_tok_spec(D_MODEL), pl.BlockSpec((None, 1, 6 * D_MODEL), lambda b, t: (b, 0, 0)),
                 _const_spec(nw.shape)]
                + [_tok_spec(LANES)] * 3
                + [_const_spec(a.shape) for a in consts]
                + [_const_spec(w.shape) for w in weights])
    return pl.pallas_call(
        kern,
        grid=(bsz, seq // TM_PROJ),
        in_specs=in_specs,
        out_specs=[_tok_spec(w) for w, _ in outs],
        out_shape=[jax.ShapeDtypeStruct((bsz, seq, w), dt) for w, dt in outs],
        compiler_params=pltpu.CompilerParams(
            dimension_semantics=("arbitrary", "arbitrary"), vmem_limit_bytes=VMEM_LIMIT),
        name=name,
    )(x, mod_l, nw, *tables, *consts, *weights)


def _gl_kernel(q_ref, k_ref, g_ref, v_ref, gate_ref, nw_ref, o_ref, st_ref, *, heads):
    @pl.when(pl.program_id(2) == 0)
    def _():
        st_ref[...] = jnp.zeros_like(st_ref)

    dk = LANES // heads
    row = lax.broadcasted_iota(jnp.int32, (CHUNK, CHUNK), 0)
    col = lax.broadcasted_iota(jnp.int32, (CHUNK, CHUNK), 1)
    causal = row >= col
    tri = jnp.where(causal, 1.0, 0.0).astype(BF16)
    lane_sub = lax.broadcasted_iota(jnp.int32, (SUB, LANES), 1)
    lane_chunk = lax.broadcasted_iota(jnp.int32, (CHUNK, LANES), 1)
    nw = nw_ref[...]
    n_sub = CHUNK // SUB
    zeros_sub = jnp.zeros((SUB, LANES), BF16)

    def chunk_body(ci, carry):
        r0 = pl.multiple_of(ci * CHUNK, CHUNK)
        rows = pl.ds(r0, CHUNK)
        q = q_ref[rows, :]
        k = k_ref[rows, :]
        g = g_ref[rows, :]
        g1 = g.astype(BF16)
        r1 = g - g1.astype(F32)
        g2 = r1.astype(BF16)
        g3 = (r1 - g2.astype(F32)).astype(BF16)
        b = _dot(tri, g1) + _dot(tri, g2) + _dot(tri, g3)
        btot = b[CHUNK - 1:CHUNK, :]
        q_in = (q * jnp.exp(b)).astype(BF16)
        k_out = k * jnp.exp(btot - b)
        decay_tot = jnp.exp(btot)

        k_groups, q_groups = [], []
        for a in range(n_sub):
            hi = SUB * (a + 1)
            lvl = b[SUB * a - 1:SUB * a, :] if a else jnp.zeros((1, LANES), F32)
            kk = (k[:hi] * jnp.exp(jnp.minimum(lvl - b[:hi], EXP_CLAMP))).astype(BF16)
            if hi < CHUNK:
                kk = jnp.concatenate([kk, jnp.zeros((CHUNK - hi, LANES), BF16)], axis=0)
            k_groups.append(kk)
            q_groups.append(q[SUB * a:hi] * jnp.exp(b[SUB * a:hi] - lvl))
        k_stack = jnp.concatenate(k_groups, axis=1)

        for p in range(heads):
            def own(t, lane):
                return t if heads == 1 else jnp.where((lane >= p * dk) & (lane < (p + 1) * dk), t, 0.0)
            q_rows = []
            for a in range(n_sub):
                qa = own(q_groups[a], lane_sub).astype(BF16)
                q_rows.append(jnp.concatenate([zeros_sub] * a + [qa] + [zeros_sub] * (n_sub - 1 - a), axis=1))
            q_stack = jnp.concatenate(q_rows, axis=0)
            s = lax.dot_general(q_stack, k_stack, NT_DIMS, preferred_element_type=F32)
            s = jnp.where(causal, s, 0.0).astype(BF16)
            v = v_ref[rows, p * LANES:(p + 1) * LANES]
            st = st_ref[p]
            o = _dot(s, v) + lax.dot_general(q_in, st.astype(BF16), NT_DIMS, preferred_element_type=F32)
            k_own = own(k_out, lane_chunk).astype(BF16)
            st_ref[p] = st * decay_tot + lax.dot_general(v, k_own, TN_DIMS, preferred_element_type=F32)
            gate = gate_ref[rows, p * LANES:(p + 1) * LANES]
            o_ref[rows, p * LANES:(p + 1) * LANES] = (_head_norm(o, nw) * gate).astype(BF16)
        return carry

    lax.fori_loop(0, q_ref.shape[0] // CHUNK, chunk_body, 0)


def _gated_linear(q, k, g, v, gate, nw, heads):
    bsz, seq, width = q.shape
    groups = width // LANES
    vw = heads * LANES
    qspec = pl.BlockSpec((None, TS_GL, LANES), lambda b, gi, s: (b, s, gi))
    vspec = pl.BlockSpec((None, TS_GL, vw), lambda b, gi, s: (b, s, gi))
    return pl.pallas_call(
        functools.partial(_gl_kernel, heads=heads),
        grid=(bsz, groups, seq // TS_GL),
        in_specs=[qspec, qspec, qspec, vspec, vspec, pl.BlockSpec((1, LANES), lambda b, gi, s: (0, 0))],
        out_specs=vspec,
        out_shape=jax.ShapeDtypeStruct((bsz, seq, groups * vw), BF16),
        scratch_shapes=[pltpu.VMEM((heads, LANES, LANES), F32)],
        compiler_params=pltpu.CompilerParams(
            dimension_semantics=("arbitrary", "arbitrary", "arbitrary"), vmem_limit_bytes=VMEM_LIMIT),
        name=f"gated_linear_{heads}",
    )(q, k, g, v, gate, nw)


def _swa_kernel(sink_ref, q_ref, cur_ref, prev_ref, o_ref):
    n = pl.program_id(1)
    cur = cur_ref[...]
    prev = prev_ref[...]
    kk = jnp.concatenate([prev[:, :LANES], cur[:, :LANES]], axis=0)
    vv = jnp.concatenate([prev[:, LANES:], cur[:, LANES:]], axis=0)
    qi = lax.broadcasted_iota(jnp.int32, (WINDOW, 2 * WINDOW), 0) + WINDOW
    kj = lax.broadcasted_iota(jnp.int32, (WINDOW, 2 * WINDOW), 1)
    rel = qi - kj
    mask = (rel >= 0) & (rel < WINDOW) & ((n > 0) | (kj >= WINDOW))
    lane = lax.broadcasted_iota(jnp.int32, (WINDOW, LANES), 1)
    low = lane < HEAD_DIM
    per_kv = N_SWA_HEADS // N_SWA_KV_HEADS
    for i in range(per_kv):
        qg = q_ref[:, i * LANES:(i + 1) * LANES]
        halves = []
        for kv in range(N_SWA_KV_HEADS):
            sink = sink_ref[kv * per_kv + i]
            qm = jnp.where(low if kv == 0 else ~low, qg, jnp.zeros_like(qg))
            s = lax.dot_general(qm, kk, NT_DIMS, preferred_element_type=F32)
            s = jnp.where(mask, s, -jnp.inf)
            m = jnp.maximum(jnp.max(s, axis=-1, keepdims=True), sink)
            p = jnp.exp(s - m)
            p = p / (jnp.sum(p, axis=-1, keepdims=True) + jnp.exp(sink - m))
            halves.append(_dot(p.astype(BF16), vv))
        o_ref[:, i * LANES:(i + 1) * LANES] = jnp.where(low, halves[0], halves[1]).astype(BF16)


def _sliding_window(sinks, q, kv):
    bsz, seq, _ = q.shape
    return pl.pallas_call(
        _swa_kernel,
        grid=(bsz, seq // WINDOW),
        in_specs=[
            pl.BlockSpec(memory_space=pltpu.SMEM),
            pl.BlockSpec((None, WINDOW, MIX_HALF), lambda b, n: (b, n, 0)),
            pl.BlockSpec((None, WINDOW, 2 * LANES), lambda b, n: (b, n, 0)),
            pl.BlockSpec((None, WINDOW, 2 * LANES), lambda b, n: (b, jnp.maximum(n - 1, 0), 0)),
        ],
        out_specs=pl.BlockSpec((None, WINDOW, MIX_HALF), lambda b, n: (b, n, 0)),
        out_shape=jax.ShapeDtypeStruct((bsz, seq, MIX_HALF), BF16),
        compiler_params=pltpu.CompilerParams(dimension_semantics=("arbitrary", "arbitrary")),
        name="sliding_window",
    )(sinks, q, kv, kv)


def _diff_kernel(lam_ref, q_ref, k_ref, v_ref, nw_ref, o_ref, *, out_scale):
    qi = pl.program_id(2)
    tq = q_ref.shape[0]
    q = q_ref[...]
    lane = lax.broadcasted_iota(jnp.int32, q.shape, 1)
    zero = jnp.zeros_like(q)
    qs = (jnp.where(lane < HEAD_DIM, q, zero), jnp.where(lane >= HEAD_DIM, q, zero))
    row = lax.broadcasted_iota(jnp.int32, (tq, tq), 0)
    col = lax.broadcasted_iota(jnp.int32, (tq, tq), 1)
    causal = row >= col

    def update(state, kt, vt, masked):
        new = []
        for (m, l, acc), qm in zip(state, qs):
            s = lax.dot_general(qm, kt, NT_DIMS, preferred_element_type=F32)
            if masked:
                s = jnp.where(causal, s, -jnp.inf)
            m_new = jnp.maximum(m, jnp.max(s, axis=-1, keepdims=True))
            alpha = jnp.exp(m - m_new)
            p = jnp.exp(s - m_new)
            l = alpha * l + jnp.sum(p, axis=-1, keepdims=True)
            acc = alpha * acc + _dot(p.astype(BF16), vt)
            new.append((m_new, l, acc))
        return tuple(new)

    def body(j, state):
        r = pl.ds(pl.multiple_of(j * tq, tq), tq)
        return update(state, k_ref[r, :], v_ref[r, :], False)

    init = tuple((jnp.full((tq, 1), -jnp.inf, F32), jnp.zeros((tq, 1), F32), jnp.zeros((tq, LANES), F32))
                 for _ in range(2))
    state = lax.fori_loop(0, qi, body, init)
    r = pl.ds(pl.multiple_of(qi * tq, tq), tq)
    (_, l1, acc1), (_, l2, acc2) = update(state, k_ref[r, :], v_ref[r, :], True)
    o = acc1 / l1 - lam_ref[...] * (acc2 / l2)
    o_ref[...] = (_head_norm(o, nw_ref[...]) * out_scale).astype(BF16)


def _differential(lam_row, q, k, v, nw, out_scale):
    bsz, seq, width = q.shape
    nh = width // LANES
    kv_spec = pl.BlockSpec((None, seq, LANES), lambda b, h, i: (b, 0, h))
    q_spec = pl.BlockSpec((None, TQ_DIFF, LANES), lambda b, h, i: (b, i, h))
    row_spec = pl.BlockSpec((1, LANES), lambda b, h, i: (0, 0))
    return pl.pallas_call(
        functools.partial(_diff_kernel, out_scale=out_scale),
        grid=(bsz, nh, seq // TQ_DIFF),
        in_specs=[row_spec, q_spec, kv_spec, kv_spec, row_spec],
        out_specs=q_spec,
        out_shape=jax.ShapeDtypeStruct((bsz, seq, width), BF16),
        compiler_params=pltpu.CompilerParams(
            dimension_semantics=("arbitrary", "arbitrary", "arbitrary"), vmem_limit_bytes=VMEM_LIMIT),
        name="differential",
    )(lam_row, q, k, v, nw)


def _out_ffn_kernel(x_ref, oa_ref, ob_ref, mod_ref, nw_ref, fw_ref, woa_ref, wob_ref,
                    wa_ref, wu_ref, cw_ref, cb_ref, wo_ref,
                    y_ref, act_ref, carry_ref, *, final):
    t = pl.program_id(1)
    m = mod_ref[...]
    d = D_MODEL
    g1, sh2, sc2, g2 = m[:, 2 * d:3 * d], m[:, 3 * d:4 * d], m[:, 4 * d:5 * d], m[:, 5 * d:6 * d]
    mix = _dot(oa_ref[...], woa_ref[...]) + _dot(ob_ref[...], wob_ref[...])
    x1 = x_ref[...] + g1 * mix
    h = _modulated_norm(x1, nw_ref[...], sc2, sh2).astype(BF16)

    @pl.when(t == 0)
    def _():
        carry_ref[...] = jnp.zeros_like(carry_ref)

    tm = x1.shape[0]
    row = lax.broadcasted_iota(jnp.int32, (tm, FF_CHUNK), 0)
    for c in range(D_FF // FF_CHUNK):
        a = _dot(h, wa_ref[c])
        u = _dot(h, wu_ref[c])
        tail = carry_ref[c]
        cw = cw_ref[c]
        prev1 = jnp.where(row < 1, tail[7:8, :], pltpu.roll(a, 1, 0))
        prev2 = jnp.where(row < 1, tail[6:7, :], jnp.where(row < 2, tail[7:8, :], pltpu.roll(a, 2, 0)))
        carry_ref[c] = a[tm - 8:tm, :]
        conv = prev2 * cw[0:1, :] + prev1 * cw[1:2, :] + a * cw[2:3, :] + cb_ref[c]
        act_ref[:, c * FF_CHUNK:(c + 1) * FF_CHUNK] = (_silu(conv) * u).astype(BF16)
    y = x1 + g2 * _dot(act_ref[...], wo_ref[...])
    if final:
        ms = jnp.mean(y * y, axis=-1, keepdims=True)
        y = (y * lax.rsqrt(ms + NORM_EPS)) * fw_ref[...]
    y_ref[...] = y


def _out_ffn(x, oa, ob, mod_l, nw, fw, woa, wob, wa, wu, cw, cb, wo, final):
    bsz, seq, _ = x.shape
    n_ch = D_FF // FF_CHUNK

    def resident(shape):
        nd = len(shape)
        return pl.BlockSpec(shape, lambda b, t: (0,) * nd, pipeline_mode=pl.Buffered(1))

    return pl.pallas_call(
        functools.partial(_out_ffn_kernel, final=final),
        grid=(bsz, seq // TM_PROJ),
        in_specs=[_tok_spec(D_MODEL), _tok_spec(MIX_HALF), _tok_spec(MIX_HALF),
                  pl.BlockSpec((None, 1, 6 * D_MODEL), lambda b, t: (b, 0, 0)),
                  _const_spec(nw.shape), _const_spec(fw.shape),
                  resident(woa.shape), resident(wob.shape), resident(wa.shape), resident(wu.shape),
                  _const_spec(cw.shape), _const_spec(cb.shape), resident(wo.shape)],
        out_specs=_tok_spec(D_MODEL),
        out_shape=jax.ShapeDtypeStruct(x.shape, F32),
        scratch_shapes=[pltpu.VMEM((TM_PROJ, D_FF), BF16), pltpu.VMEM((n_ch, 8, FF_CHUNK), F32)],
        compiler_params=pltpu.CompilerParams(
            dimension_semantics=("arbitrary", "arbitrary"), vmem_limit_bytes=VMEM_LIMIT),
        name="out_ffn",
    )(x, oa, ob, mod_l, nw, fw, woa, wob, wa, wu, cw, cb, wo)


def _rope_lane_tables(positions):
    half = ROT_DIMS // 2
    inv_freq = ROPE_THETA ** (-jnp.arange(0, ROT_DIMS, 2, dtype=F32) / ROT_DIMS)
    ang = positions.astype(F32)[..., None] * inv_freq
    cos, sin = jnp.cos(ang), jnp.sin(ang)
    shp = cos.shape[:-1] + (HEAD_DIM - ROT_DIMS,)
    c = jnp.concatenate([cos, cos, jnp.ones(shp, F32)], axis=-1)
    s1 = jnp.concatenate([-sin, jnp.zeros(cos.shape[:-1] + (HEAD_DIM - half,), F32)], axis=-1)
    s2 = jnp.concatenate([jnp.zeros_like(sin), sin, jnp.zeros(shp, F32)], axis=-1)
    return tuple(jnp.concatenate([t, t], axis=-1) for t in (c, s1, s2))


def _cols(w, start, width):
    return w[:, start:start + width].astype(BF16)


SWA_HEAD_ORDER = (0, 4, 1, 5, 2, 6, 3, 7)


def kernel(x, c, positions, mod_w, mod_b, norm_mix_w, norm_ffn_w, ev_w_in, gla_gate_w, gla_gate_b,
           gla_norm_w, swa_sinks, ev_w_out, od_w_in, diff_lambda, diff_norm_w, hgrn_lb_logits,
           hgrn_norm_w, od_w_out, ffn_w_in, ffn_conv_w, ffn_conv_b, ffn_w_out, final_norm_w):
    bsz, seq, _ = x.shape
    tables = _rope_lane_tables(positions)
    lbs = jnp.cumsum(jax.nn.softmax(hgrn_lb_logits.astype(F32), axis=0), axis=0)
    lbs = lbs - lbs[0:1]
    mod = _modulation(c, mod_w, mod_b).reshape(DEPTH, bsz, 1, 6 * D_MODEL)
    n_ch = D_FF // FF_CHUNK
    swa_cols = jnp.concatenate([jnp.arange(HEAD_DIM) + HEAD_DIM * h for h in SWA_HEAD_ORDER])
    fw = final_norm_w.reshape(1, D_MODEL)

    for l in range(DEPTH):
        j = l // 2
        nw_mix = norm_mix_w[l].reshape(1, D_MODEL)
        if l % 2 == 0:
            w = ev_w_in[j]
            w_sq = w[:, 1552:2064][:, swa_cols].astype(BF16)
            w_glr = jnp.pad(w[:, 1536:1552], ((0, 0), (0, LANES - GLA_GATE_RANK))).astype(BF16)
            gate_w = jnp.pad(gla_gate_w[j], ((0, LANES - GLA_GATE_RANK), (0, 0))).astype(BF16)
            weights = [_cols(w, 0, 256), _cols(w, 256, 256), _cols(w, 512, 512), _cols(w, 1024, 512),
                       w_glr, gate_w, gla_gate_b[j].reshape(1, -1), w_sq, _cols(w, 2064, 256)]
            outs = [(256, F32), (256, F32), (512, BF16), (512, F32), (256, F32), (512, BF16), (256, BF16)]
            gq, gk, gv, gr, ga, sq, skv = _in_projection(
                _even_in_kernel, "even_in", x, mod[l], nw_mix, tables, [], weights, outs)
            oa = _gated_linear(gq, gk, ga, gv, gr, gla_norm_w[j].reshape(1, LANES), heads=2)
            ob = _sliding_window(swa_sinks[j], sq, skv)
            w_out = ev_w_out[j]
            wob = w_out[MIX_HALF:][swa_cols].astype(BF16)
        else:
            w = od_w_in[j]
            lb = lbs[j].reshape(1, -1)
            consts = [jnp.log(lb), jnp.log1p(-lb), 1.0 - lb]
            weights = [_cols(w, 512 * i, 512) for i in range(7)]
            outs = [(512, BF16), (512, BF16), (512, BF16), (512, F32), (512, F32), (512, F32),
                    (512, BF16), (512, F32)]
            dq, dk, dv, hq, hk, hf, hi, hg = _in_projection(
                _odd_in_kernel, "odd_in", x, mod[l], nw_mix, tables, consts, weights, outs)
            lam_init = 0.8 - 0.6 * math.exp(-0.3 * l)
            lv = diff_lambda[j].astype(F32)
            lam = jnp.exp(jnp.sum(lv[0] * lv[1])) - jnp.exp(jnp.sum(lv[2] * lv[3])) + lam_init
            lam_row = jnp.full((1, LANES), lam, F32)
            oa = _differential(lam_row, dq, dk, dv, diff_norm_w[j].reshape(1, LANES), 1.0 - lam_init)
            ob = _gated_linear(hq, hk, hf, hi, hg, hgrn_norm_w[j].reshape(1, LANES), heads=1)
            w_out = od_w_out[j]
            wob = w_out[MIX_HALF:].astype(BF16)
        woa = w_out[:MIX_HALF].astype(BF16)
        wi = ffn_w_in[l]
        wa = wi[:, :D_FF].reshape(D_MODEL, n_ch, FF_CHUNK).transpose(1, 0, 2).astype(BF16)
        wu = wi[:, D_FF:].reshape(D_MODEL, n_ch, FF_CHUNK).transpose(1, 0, 2).astype(BF16)
        cw = ffn_conv_w[l].reshape(3, n_ch, FF_CHUNK).transpose(1, 0, 2)
        cb = ffn_conv_b[l].reshape(n_ch, 1, FF_CHUNK)
        x = _out_ffn(x, oa, ob, mod[l], norm_ffn_w[l].reshape(1, D_MODEL), fw, woa, wob,
                     wa, wu, cw, cb, ffn_w_out[l].astype(BF16), final=(l == DEPTH - 1))
    return x
```

```python
import functools
import math

import jax
import jax.numpy as jnp
from jax import lax
from jax.experimental import pallas as pl
from jax.experimental.pallas import tpu as pltpu

F32 = jnp.float32
BF16 = jnp.bfloat16

D_MODEL = 1024
DEPTH = 4
HEAD_DIM = 64
ROT_DIMS = HEAD_DIM // 4
ROPE_THETA = 500000.0
GLA_GATE_RANK = 16
GLA_GATE_NORMALIZER = 16.0
N_SWA_HEADS = 8
N_SWA_KV_HEADS = 2
WINDOW = 128
HGRN_EXPAND = 128
D_FF = 2816
MIX_HALF = D_MODEL // 2
NORM_EPS = 1e-6
LOG2_E = math.log2(math.e)

LANES = 128
CHUNK = 64
SUB = 16
EXP_CLAMP = 80.0
TM_PROJ = 512
TS_GL = 512
TQ_DIFF = 256
FF_CHUNK = 256
VMEM_LIMIT = 56 * 1024 * 1024

NT_DIMS = (((1,), (1,)), ((), ()))
TN_DIMS = (((0,), (0,)), ((), ()))


def _sigmoid(z):
    return 1.0 / (1.0 + jnp.exp(-z))


def _silu(z):
    return z * _sigmoid(z)


def _log_sigmoid(z):
    return jnp.minimum(z, 0.0) - jnp.log(1.0 + jnp.exp(-jnp.abs(z)))


def _dot(a, b):
    return jnp.dot(a, b, preferred_element_type=F32)


def _modulated_norm(x, nw, sc, sh):
    ms = jnp.mean(x * x, axis=-1, keepdims=True)
    return ((x * lax.rsqrt(ms + NORM_EPS)) * nw) * (1.0 + sc) + sh


def _head_norm(o, nw):
    ms = jnp.mean(o * o, axis=-1, keepdims=True)
    return (o * lax.rsqrt(ms + NORM_EPS)) * nw


def _rope(x, c, s1, s2):
    n = x.shape[-1] // LANES
    if n > 1:
        c, s1, s2 = (jnp.concatenate([t] * n, axis=-1) for t in (c, s1, s2))
    half = ROT_DIMS // 2
    up = pltpu.roll(x, x.shape[-1] - half, 1)
    down = pltpu.roll(x, half, 1)
    return x * c + up * s1 + down * s2


def _mod_kernel(c_ref, w_ref, b_ref, o_ref):
    ca = _silu(c_ref[...]).astype(BF16)
    o_ref[0] = _dot(ca, w_ref[0].astype(BF16)) + b_ref[0]


def _modulation(c, mod_w, mod_b):
    bsz = c.shape[0]
    tn = 1024
    n_out = mod_w.shape[-1]
    return pl.pallas_call(
        _mod_kernel,
        grid=(DEPTH, n_out // tn),
        in_specs=[
            pl.BlockSpec((bsz, D_MODEL), lambda l, j: (0, 0)),
            pl.BlockSpec((1, D_MODEL, tn), lambda l, j: (l, 0, j)),
            pl.BlockSpec((1, 1, tn), lambda l, j: (l, 0, j)),
        ],
        out_specs=pl.BlockSpec((1, bsz, tn), lambda l, j: (l, 0, j)),
        out_shape=jax.ShapeDtypeStruct((DEPTH, bsz, n_out), F32),
        name="modulation",
    )(c, mod_w, mod_b.reshape(DEPTH, 1, n_out))


def _even_in_kernel(x_ref, mod_ref, nw_ref, rc_ref, rs1_ref, rs2_ref,
                    wq_ref, wk_ref, wv_ref, wr_ref, wl_ref, gw_ref, gb_ref, wsq_ref, wskv_ref,
                    gq_ref, gk_ref, gv_ref, gr_ref, ga_ref, sq_ref, skv_ref):
    m = mod_ref[...]
    h = _modulated_norm(x_ref[...], nw_ref[...], m[:, D_MODEL:2 * D_MODEL], m[:, 0:D_MODEL]).astype(BF16)
    gq_ref[...] = _dot(h, wq_ref[...]) * (HEAD_DIM ** -0.5)
    gk_ref[...] = _dot(h, wk_ref[...])
    gv_ref[...] = _dot(h, wv_ref[...]).astype(BF16)
    gr_ref[...] = _silu(_dot(h, wr_ref[...]))
    glr = _dot(h, wl_ref[...]).astype(BF16)
    gate = _dot(glr, gw_ref[...]) + gb_ref[...]
    ga_ref[...] = _log_sigmoid(gate) / GLA_GATE_NORMALIZER
    c, s1, s2 = rc_ref[...], rs1_ref[...], rs2_ref[...]
    sq = _dot(h, wsq_ref[...])
    sq_ref[...] = (_rope(sq, c, s1, s2) * (HEAD_DIM ** -0.5)).astype(BF16)
    skv = _dot(h, wskv_ref[...])
    sk = _rope(skv[:, :LANES], c, s1, s2)
    skv_ref[...] = jnp.concatenate([sk, skv[:, LANES:]], axis=-1).astype(BF16)


def _odd_in_kernel(x_ref, mod_ref, nw_ref, rc_ref, rs1_ref, rs2_ref, llb_ref, l1m_ref, oml_ref,
                   wdq_ref, wdk_ref, wdv_ref, whq_ref, whf_ref, whi_ref, whg_ref,
                   dq_ref, dk_ref, dv_ref, hq_ref, hk_ref, hf_ref, hi_ref, hg_ref):
    m = mod_ref[...]
    h = _modulated_norm(x_ref[...], nw_ref[...], m[:, D_MODEL:2 * D_MODEL], m[:, 0:D_MODEL]).astype(BF16)
    c, s1, s2 = rc_ref[...], rs1_ref[...], rs2_ref[...]
    dq_ref[...] = (_rope(_dot(h, wdq_ref[...]), c, s1, s2) * (LOG2_E * HEAD_DIM ** -0.5)).astype(BF16)
    dk_ref[...] = _rope(_dot(h, wdk_ref[...]), c, s1, s2).astype(BF16)
    dv_ref[...] = _dot(h, wdv_ref[...]).astype(BF16)
    hq_ref[...] = _silu(_dot(h, whq_ref[...])) * (HGRN_EXPAND ** -0.5)
    z = _dot(h, whf_ref[...])
    a1 = llb_ref[...]
    a2 = l1m_ref[...] + _log_sigmoid(z)
    hf_ref[...] = jnp.maximum(a1, a2) + jnp.log(1.0 + jnp.exp(-jnp.abs(a1 - a2)))
    hk_ref[...] = oml_ref[...] * _sigmoid(-z)
    hi_ref[...] = _dot(h, whi_ref[...]).astype(BF16)
    hg_ref[...] = _silu(_dot(h, whg_ref[...]))


def _tok_spec(width):
    return pl.BlockSpec((None, TM_PROJ, width), lambda b, t: (b, t, 0))


def _const_spec(shape):
    nd = len(shape)
    return pl.BlockSpec(shape, lambda b, t: (0,) * nd)


def _in_projection(kern, name, x, mod_l, nw, tables, consts, weights, outs):
    bsz, seq, _ = x.shape
    in_specs = ([_tok_spec(D_MODEL), pl.BlockSpec((None, 1, 6 * D_MODEL), lambda b, t: (b, 0, 0)),
                 _const_spec(nw.shape)]
                + [_tok_spec(LANES)] * 3
                + [_const_spec(a.shape) for a in consts]
                + [_const_spec(w.shape) for w in weights])
    return pl.pallas_call(
        kern,
        grid=(bsz, seq // TM_PROJ),
        in_specs=in_specs,
        out_specs=[_tok_spec(w) for w, _ in outs],
        out_shape=[jax.ShapeDtypeStruct((bsz, seq, w), dt) for w, dt in outs],
        compiler_params=pltpu.CompilerParams(
            dimension_semantics=("arbitrary", "arbitrary"), vmem_limit_bytes=VMEM_LIMIT),
        name=name,
    )(x, mod_l, nw, *tables, *consts, *weights)


def _gl_kernel(q_ref, k_ref, g_ref, v_ref, gate_ref, nw_ref, o_ref, st_ref, *, heads):
    @pl.when(pl.program_id(2) == 0)
    def _():
        st_ref[...] = jnp.zeros_like(st_ref)

    dk = LANES // heads
    row = lax.broadcasted_iota(jnp.int32, (CHUNK, CHUNK), 0)
    col = lax.broadcasted_iota(jnp.int32, (CHUNK, CHUNK), 1)
    causal = row >= col
    tri = jnp.where(causal, 1.0, 0.0).astype(BF16)
    lane_sub = lax.broadcasted_iota(jnp.int32, (SUB, LANES), 1)
    lane_chunk = lax.broadcasted_iota(jnp.int32, (CHUNK, LANES), 1)
    nw = nw_ref[...]
    n_sub = CHUNK // SUB
    zeros_sub = jnp.zeros((SUB, LANES), BF16)

    def own(t, lane, p):
        return t if heads == 1 else jnp.where((lane >= p * dk) & (lane < (p + 1) * dk), t, 0.0)

    chunks = [slice(c * CHUNK, (c + 1) * CHUNK) for c in range(q_ref.shape[0] // CHUNK)]
    heads_r = range(heads)

    cum = []
    for rows in chunks:
        g = g_ref[rows, :]
        g1 = g.astype(BF16)
        r1 = g - g1.astype(F32)
        g2 = r1.astype(BF16)
        g3 = (r1 - g2.astype(F32)).astype(BF16)
        cum.append(_dot(tri, g1) + _dot(tri, g2) + _dot(tri, g3))

    q_in, decay_tot, k_own, k_stack, q_stack = [], [], [], [], []
    for rows, b in zip(chunks, cum):
        q = q_ref[rows, :]
        k = k_ref[rows, :]
        btot = b[CHUNK - 1:CHUNK, :]
        q_in.append((q * jnp.exp(b)).astype(BF16))
        k_out = k * jnp.exp(btot - b)
        k_own.append([own(k_out, lane_chunk, p).astype(BF16) for p in heads_r])
        decay_tot.append(jnp.exp(btot))
        k_groups, q_groups = [], []
        for a in range(n_sub):
            hi = SUB * (a + 1)
            lvl = b[SUB * a - 1:SUB * a, :] if a else jnp.zeros((1, LANES), F32)
            kk = (k[:hi] * jnp.exp(jnp.minimum(lvl - b[:hi], EXP_CLAMP))).astype(BF16)
            if hi < CHUNK:
                kk = jnp.concatenate([kk, jnp.zeros((CHUNK - hi, LANES), BF16)], axis=0)
            k_groups.append(kk)
            q_groups.append(q[SUB * a:hi] * jnp.exp(b[SUB * a:hi] - lvl))
        k_stack.append(jnp.concatenate(k_groups, axis=1))
        per_head = []
        for p in heads_r:
            q_rows = []
            for a in range(n_sub):
                qa = own(q_groups[a], lane_sub, p).astype(BF16)
                q_rows.append(jnp.concatenate([zeros_sub] * a + [qa] + [zeros_sub] * (n_sub - 1 - a), axis=1))
            per_head.append(jnp.concatenate(q_rows, axis=0))
        q_stack.append(per_head)

    vals = [[v_ref[rows, p * LANES:(p + 1) * LANES] for p in heads_r] for rows in chunks]
    scores = [[lax.dot_general(q_stack[c][p], k_stack[c], NT_DIMS, preferred_element_type=F32)
               for p in heads_r] for c in range(len(chunks))]
    incr = [[lax.dot_general(vals[c][p], k_own[c][p], TN_DIMS, preferred_element_type=F32)
             for p in heads_r] for c in range(len(chunks))]

    outs = []
    for p in heads_r:
        st = st_ref[p]
        for c in range(len(chunks)):
            s = jnp.where(causal, scores[c][p], 0.0).astype(BF16)
            o = _dot(s, vals[c][p]) + lax.dot_general(q_in[c], st.astype(BF16), NT_DIMS,
                                                     preferred_element_type=F32)
            outs.append((c, p, o))
            st = st * decay_tot[c] + incr[c][p]
        st_ref[p] = st

    for c, p, o in outs:
        cols = slice(p * LANES, (p + 1) * LANES)
        o_ref[chunks[c], cols] = (_head_norm(o, nw) * gate_ref[chunks[c], cols]).astype(BF16)


def _gated_linear(q, k, g, v, gate, nw, heads):
    bsz, seq, width = q.shape
    groups = width // LANES
    vw = heads * LANES
    qspec = pl.BlockSpec((None, TS_GL, LANES), lambda b, gi, s: (b, s, gi))
    vspec = pl.BlockSpec((None, TS_GL, vw), lambda b, gi, s: (b, s, gi))
    return pl.pallas_call(
        functools.partial(_gl_kernel, heads=heads),
        grid=(bsz, groups, seq // TS_GL),
        in_specs=[qspec, qspec, qspec, vspec, vspec, pl.BlockSpec((1, LANES), lambda b, gi, s: (0, 0))],
        out_specs=vspec,
        out_shape=jax.ShapeDtypeStruct((bsz, seq, groups * vw), BF16),
        scratch_shapes=[pltpu.VMEM((heads, LANES, LANES), F32)],
        compiler_params=pltpu.CompilerParams(
            dimension_semantics=("arbitrary", "arbitrary", "arbitrary"), vmem_limit_bytes=VMEM_LIMIT),
        name=f"gated_linear_{heads}",
    )(q, k, g, v, gate, nw)


def _swa_kernel(sink_ref, q_ref, cur_ref, prev_ref, o_ref):
    n = pl.program_id(1)
    cur = cur_ref[...]
    prev = prev_ref[...]
    kk = jnp.concatenate([prev[:, :LANES], cur[:, :LANES]], axis=0)
    vv = jnp.concatenate([prev[:, LANES:], cur[:, LANES:]], axis=0)
    qi = lax.broadcasted_iota(jnp.int32, (WINDOW, 2 * WINDOW), 0) + WINDOW
    kj = lax.broadcasted_iota(jnp.int32, (WINDOW, 2 * WINDOW), 1)
    rel = qi - kj
    mask = (rel >= 0) & (rel < WINDOW) & ((n > 0) | (kj >= WINDOW))
    lane = lax.broadcasted_iota(jnp.int32, (WINDOW, LANES), 1)
    low = lane < HEAD_DIM
    per_kv = N_SWA_HEADS // N_SWA_KV_HEADS
    for i in range(per_kv):
        qg = q_ref[:, i * LANES:(i + 1) * LANES]
        halves = []
        for kv in range(N_SWA_KV_HEADS):
            sink = sink_ref[kv * per_kv + i]
            qm = jnp.where(low if kv == 0 else ~low, qg, jnp.zeros_like(qg))
            s = lax.dot_general(qm, kk, NT_DIMS, preferred_element_type=F32)
            s = jnp.where(mask, s, -jnp.inf)
            m = jnp.maximum(jnp.max(s, axis=-1, keepdims=True), sink)
            p = jnp.exp(s - m)
            p = p / (jnp.sum(p, axis=-1, keepdims=True) + jnp.exp(sink - m))
            halves.append(_dot(p.astype(BF16), vv))
        o_ref[:, i * LANES:(i + 1) * LANES] = jnp.where(low, halves[0], halves[1]).astype(BF16)


def _sliding_window(sinks, q, kv):
    bsz, seq, _ = q.shape
    return pl.pallas_call(
        _swa_kernel,
        grid=(bsz, seq // WINDOW),
        in_specs=[
            pl.BlockSpec(memory_space=pltpu.SMEM),
            pl.BlockSpec((None, WINDOW, MIX_HALF), lambda b, n: (b, n, 0)),
            pl.BlockSpec((None, WINDOW, 2 * LANES), lambda b, n: (b, n, 0)),
            pl.BlockSpec((None, WINDOW, 2 * LANES), lambda b, n: (b, jnp.maximum(n - 1, 0), 0)),
        ],
        out_specs=pl.BlockSpec((None, WINDOW, MIX_HALF), lambda b, n: (b, n, 0)),
        out_shape=jax.ShapeDtypeStruct((bsz, seq, MIX_HALF), BF16),
        compiler_params=pltpu.CompilerParams(dimension_semantics=("arbitrary", "arbitrary")),
        name="sliding_window",
    )(sinks, q, kv, kv)


def _diff_kernel(lam_ref, q_ref, k_ref, vt_ref, nw_ref, o_ref, acc_ref, stat_ref, *, out_scale):
    qi = pl.program_id(1)
    tq = q_ref.shape[0]
    nh = q_ref.shape[1] // LANES
    lane = lax.broadcasted_iota(jnp.int32, (tq, LANES), 1)
    qs = []
    for h in range(nh):
        q = q_ref[:, h * LANES:(h + 1) * LANES]
        zero = jnp.zeros_like(q)
        qs += [jnp.where(lane < HEAD_DIM, q, zero), jnp.where(lane >= HEAD_DIM, q, zero)]
    key = lax.broadcasted_iota(jnp.int32, (tq, tq), 0)
    qry = lax.broadcasted_iota(jnp.int32, (tq, tq), 1)
    causal = key <= qry
    acc_ref[...] = jnp.zeros_like(acc_ref)
    for c in range(2 * nh):
        stat_ref[2 * c:2 * c + 1, :] = jnp.full((1, tq), -jnp.inf, F32)
        stat_ref[2 * c + 1:2 * c + 2, :] = jnp.zeros((1, tq), F32)

    def update(j, masked):
        rows = pl.ds(pl.multiple_of(j * tq, tq), tq)
        scores = [lax.dot_general(k_ref[rows, (c // 2) * LANES:(c // 2 + 1) * LANES], qs[c], NT_DIMS,
                                  preferred_element_type=F32) for c in range(2 * nh)]
        probs, alphas = [], []
        for c, s in enumerate(scores):
            if masked:
                s = jnp.where(causal, s, -jnp.inf)
            m = stat_ref[2 * c:2 * c + 1, :]
            m_new = jnp.maximum(m, jnp.max(s, axis=0, keepdims=True))
            alpha = jnp.exp2(m - m_new)
            p = jnp.exp2(s - m_new)
            stat_ref[2 * c:2 * c + 1, :] = m_new
            stat_ref[2 * c + 1:2 * c + 2, :] = (alpha * stat_ref[2 * c + 1:2 * c + 2, :]
                                                + jnp.sum(p, axis=0, keepdims=True))
            probs.append(p.astype(BF16))
            alphas.append(alpha)
        for c in range(2 * nh):
            acc_ref[c] = alphas[c] * acc_ref[c] + _dot(vt_ref[j, c // 2], probs[c])

    def body(j, carry):
        update(j, False)
        return carry

    lax.fori_loop(0, qi, body, 0)
    update(qi, True)
    lam = lam_ref[0]
    nw = nw_ref[...]
    for h in range(nh):
        l1 = stat_ref[4 * h + 1:4 * h + 2, :]
        l2 = stat_ref[4 * h + 3:4 * h + 4, :]
        o_t = acc_ref[2 * h] / l1 - lam * (acc_ref[2 * h + 1] / l2)
        o_ref[:, h * LANES:(h + 1) * LANES] = (_head_norm(o_t.T, nw) * out_scale).astype(BF16)


def _differential(lam, q, k, v, nw, out_scale):
    bsz, seq, width = q.shape
    nh = width // LANES
    nt = seq // TQ_DIFF
    vt = v.reshape(bsz, nt, TQ_DIFF, nh, LANES).transpose(0, 1, 3, 4, 2)
    q_spec = pl.BlockSpec((None, TQ_DIFF, width), lambda b, i: (b, i, 0))
    return pl.pallas_call(
        functools.partial(_diff_kernel, out_scale=out_scale),
        grid=(bsz, nt),
        in_specs=[pl.BlockSpec(memory_space=pltpu.SMEM), q_spec,
                  pl.BlockSpec((None, seq, width), lambda b, i: (b, 0, 0)),
                  pl.BlockSpec((None, nt, nh, LANES, TQ_DIFF), lambda b, i: (b, 0, 0, 0, 0)),
                  pl.BlockSpec((1, LANES), lambda b, i: (0, 0))],
        out_specs=q_spec,
        out_shape=jax.ShapeDtypeStruct((bsz, seq, width), BF16),
        scratch_shapes=[pltpu.VMEM((2 * nh, LANES, TQ_DIFF), F32), pltpu.VMEM((4 * nh, TQ_DIFF), F32)],
        compiler_params=pltpu.CompilerParams(
            dimension_semantics=("arbitrary", "arbitrary"), vmem_limit_bytes=VMEM_LIMIT),
        name="differential",
    )(lam, q, k, vt, nw)


def _out_ffn_kernel(x_ref, oa_ref, ob_ref, mod_ref, nw_ref, fw_ref, woa_ref, wob_ref,
                    wi_ref, cw_ref, cb_ref, wo_ref,
                    y_ref, act_ref, carry_ref, *, final):
    t = pl.program_id(1)
    m = mod_ref[...]
    d = D_MODEL
    g1, sh2, sc2, g2 = m[:, 2 * d:3 * d], m[:, 3 * d:4 * d], m[:, 4 * d:5 * d], m[:, 5 * d:6 * d]
    mix = _dot(oa_ref[...], woa_ref[...]) + _dot(ob_ref[...], wob_ref[...])
    x1 = x_ref[...] + g1 * mix
    h = _modulated_norm(x1, nw_ref[...], sc2, sh2).astype(BF16)

    @pl.when(t == 0)
    def _():
        carry_ref[...] = jnp.zeros_like(carry_ref)

    tm = x1.shape[0]
    row = lax.broadcasted_iota(jnp.int32, (tm, FF_CHUNK), 0)
    for c in range(D_FF // FF_CHUNK):
        cols = slice(c * FF_CHUNK, (c + 1) * FF_CHUNK)
        a = _dot(h, wi_ref[:, cols])
        u = _dot(h, wi_ref[:, D_FF + c * FF_CHUNK:D_FF + (c + 1) * FF_CHUNK])
        tail = carry_ref[:, cols]
        cw = cw_ref[:, cols]
        prev1 = jnp.where(row < 1, tail[7:8, :], pltpu.roll(a, 1, 0))
        prev2 = jnp.where(row < 1, tail[6:7, :], jnp.where(row < 2, tail[7:8, :], pltpu.roll(a, 2, 0)))
        carry_ref[:, cols] = a[tm - 8:tm, :]
        conv = prev2 * cw[0:1, :] + prev1 * cw[1:2, :] + a * cw[2:3, :] + cb_ref[:, cols]
        act_ref[:, cols] = (_silu(conv) * u).astype(BF16)
    y = x1 + g2 * _dot(act_ref[...], wo_ref[...])
    if final:
        ms = jnp.mean(y * y, axis=-1, keepdims=True)
        y = (y * lax.rsqrt(ms + NORM_EPS)) * fw_ref[...]
    y_ref[...] = y


def _out_ffn(x, oa, ob, mod_l, nw, fw, woa, wob, wi, cw, cb, wo, final):
    bsz, seq, _ = x.shape

    def resident(shape):
        nd = len(shape)
        return pl.BlockSpec(shape, lambda b, t: (0,) * nd, pipeline_mode=pl.Buffered(1))

    return pl.pallas_call(
        functools.partial(_out_ffn_kernel, final=final),
        grid=(bsz, seq // TM_PROJ),
        in_specs=[_tok_spec(D_MODEL), _tok_spec(MIX_HALF), _tok_spec(MIX_HALF),
                  pl.BlockSpec((None, 1, 6 * D_MODEL), lambda b, t: (b, 0, 0)),
                  _const_spec(nw.shape), _const_spec(fw.shape),
                  resident(woa.shape), resident(wob.shape), resident(wi.shape),
                  _const_spec(cw.shape), _const_spec(cb.shape), resident(wo.shape)],
        out_specs=_tok_spec(D_MODEL),
        out_shape=jax.ShapeDtypeStruct(x.shape, F32),
        scratch_shapes=[pltpu.VMEM((TM_PROJ, D_FF), BF16), pltpu.VMEM((8, D_FF), F32)],
        compiler_params=pltpu.CompilerParams(
            dimension_semantics=("arbitrary", "arbitrary"), vmem_limit_bytes=VMEM_LIMIT),
        name="out_ffn",
    )(x, oa, ob, mod_l, nw, fw, woa, wob, wi, cw, cb, wo)


def _rope_lane_tables(positions):
    half = ROT_DIMS // 2
    inv_freq = ROPE_THETA ** (-jnp.arange(0, ROT_DIMS, 2, dtype=F32) / ROT_DIMS)
    ang = positions.astype(F32)[..., None] * inv_freq
    cos, sin = jnp.cos(ang), jnp.sin(ang)
    shp = cos.shape[:-1] + (HEAD_DIM - ROT_DIMS,)
    c = jnp.concatenate([cos, cos, jnp.ones(shp, F32)], axis=-1)
    s1 = jnp.concatenate([-sin, jnp.zeros(cos.shape[:-1] + (HEAD_DIM - half,), F32)], axis=-1)
    s2 = jnp.concatenate([jnp.zeros_like(sin), sin, jnp.zeros(shp, F32)], axis=-1)
    return tuple(jnp.concatenate([t, t], axis=-1) for t in (c, s1, s2))


def _cols(w, start, width):
    return w[:, start:start + width].astype(BF16)


SWA_HEAD_ORDER = (0, 4, 1, 5, 2, 6, 3, 7)


def kernel(x, c, positions, mod_w, mod_b, norm_mix_w, norm_ffn_w, ev_w_in, gla_gate_w, gla_gate_b,
           gla_norm_w, swa_sinks, ev_w_out, od_w_in, diff_lambda, diff_norm_w, hgrn_lb_logits,
           hgrn_norm_w, od_w_out, ffn_w_in, ffn_conv_w, ffn_conv_b, ffn_w_out, final_norm_w):
    bsz, seq, _ = x.shape
    tables = _rope_lane_tables(positions)
    lbs = jnp.cumsum(jax.nn.softmax(hgrn_lb_logits.astype(F32), axis=0), axis=0)
    lbs = lbs - lbs[0:1]
    mod = _modulation(c, mod_w, mod_b).reshape(DEPTH, bsz, 1, 6 * D_MODEL)
    swa_cols = jnp.concatenate([jnp.arange(HEAD_DIM) + HEAD_DIM * h for h in SWA_HEAD_ORDER])
    fw = final_norm_w.reshape(1, D_MODEL)

    for l in range(DEPTH):
        j = l // 2
        nw_mix = norm_mix_w[l].reshape(1, D_MODEL)
        if l % 2 == 0:
            w = ev_w_in[j]
            w_sq = w[:, 1552:2064][:, swa_cols].astype(BF16)
            w_glr = jnp.pad(w[:, 1536:1552], ((0, 0), (0, LANES - GLA_GATE_RANK))).astype(BF16)
            gate_w = jnp.pad(gla_gate_w[j], ((0, LANES - GLA_GATE_RANK), (0, 0))).astype(BF16)
            weights = [_cols(w, 0, 256), _cols(w, 256, 256), _cols(w, 512, 512), _cols(w, 1024, 512),
                       w_glr, gate_w, gla_gate_b[j].reshape(1, -1), w_sq, _cols(w, 2064, 256)]
            outs = [(256, F32), (256, F32), (512, BF16), (512, F32), (256, F32), (512, BF16), (256, BF16)]
            gq, gk, gv, gr, ga, sq, skv = _in_projection(
                _even_in_kernel, "even_in", x, mod[l], nw_mix, tables, [], weights, outs)
            oa = _gated_linear(gq, gk, ga, gv, gr, gla_norm_w[j].reshape(1, LANES), heads=2)
            ob = _sliding_window(swa_sinks[j], sq, skv)
            w_out = ev_w_out[j]
            wob = w_out[MIX_HALF:][swa_cols].astype(BF16)
        else:
            w = od_w_in[j]
            lb = lbs[j].reshape(1, -1)
            consts = [jnp.log(lb), jnp.log1p(-lb), 1.0 - lb]
            weights = [_cols(w, 512 * i, 512) for i in range(7)]
            outs = [(512, BF16), (512, BF16), (512, BF16), (512, F32), (512, F32), (512, F32),
                    (512, BF16), (512, F32)]
            dq, dk, dv, hq, hk, hf, hi, hg = _in_projection(
                _odd_in_kernel, "odd_in", x, mod[l], nw_mix, tables, consts, weights, outs)
            lam_init = 0.8 - 0.6 * math.exp(-0.3 * l)
            lv = diff_lambda[j].astype(F32)
            lam = jnp.exp(jnp.sum(lv[0] * lv[1])) - jnp.exp(jnp.sum(lv[2] * lv[3])) + lam_init
            oa = _differential(lam.reshape(1), dq, dk, dv, diff_norm_w[j].reshape(1, LANES), 1.0 - lam_init)
            ob = _gated_linear(hq, hk, hf, hi, hg, hgrn_norm_w[j].reshape(1, LANES), heads=1)
            w_out = od_w_out[j]
            wob = w_out[MIX_HALF:].astype(BF16)
        woa = w_out[:MIX_HALF].astype(BF16)
        x = _out_ffn(x, oa, ob, mod[l], norm_ffn_w[l].reshape(1, D_MODEL), fw, woa, wob,
                     ffn_w_in[l].astype(BF16), ffn_conv_w[l], ffn_conv_b[l].reshape(1, D_FF),
                     ffn_w_out[l].astype(BF16), final=(l == DEPTH - 1))
    return x
```

```python
import functools
import math

import jax
import numpy as np
import jax.numpy as jnp
from jax import lax
from jax.experimental import pallas as pl
from jax.experimental.pallas import tpu as pltpu

F32 = jnp.float32
BF16 = jnp.bfloat16

D_MODEL = 1024
DEPTH = 4
HEAD_DIM = 64
ROT_DIMS = HEAD_DIM // 4
ROPE_THETA = 500000.0
GLA_GATE_RANK = 16
GLA_GATE_NORMALIZER = 16.0
N_SWA_HEADS = 8
N_SWA_KV_HEADS = 2
WINDOW = 128
HGRN_EXPAND = 128
D_FF = 2816
MIX_HALF = D_MODEL // 2
NORM_EPS = 1e-6
LOG2_E = math.log2(math.e)

LANES = 128
CHUNK = 64
SUB = 16
EXP_CLAMP = 60.0
TM_PROJ = 512
TS_GL = 512
TQ_DIFF = 256
FF_CHUNK = 256
VMEM_LIMIT = 56 * 1024 * 1024

NT_DIMS = (((1,), (1,)), ((), ()))
TN_DIMS = (((0,), (0,)), ((), ()))


def _sigmoid(z):
    return 1.0 / (1.0 + jnp.exp(-z))


def _silu(z):
    return z * _sigmoid(z)


def _log_sigmoid(z):
    return jnp.minimum(z, 0.0) - jnp.log(1.0 + jnp.exp(-jnp.abs(z)))


def _dot(a, b):
    return jnp.dot(a, b, preferred_element_type=F32)


def _modulated_norm(x, nw, sc, sh):
    ms = jnp.mean(x * x, axis=-1, keepdims=True)
    return ((x * lax.rsqrt(ms + NORM_EPS)) * nw) * (1.0 + sc) + sh


def _head_norm(o, nw):
    ms = jnp.mean(o * o, axis=-1, keepdims=True)
    return (o * lax.rsqrt(ms + NORM_EPS)) * nw


def _rope_tables(tab):
    c, s = tab[:, :LANES], tab[:, LANES:]
    lane = lax.broadcasted_iota(jnp.int32, c.shape, 1)
    first_half = (lane & (HEAD_DIM - 1)) < ROT_DIMS // 2
    return c, jnp.where(first_half, s, 0.0), jnp.where(first_half, 0.0, s)


def _rope(x, tables):
    n = x.shape[-1] // LANES
    c, s_up, s_down = (jnp.concatenate([t] * n, axis=-1) if n > 1 else t for t in tables)
    half = ROT_DIMS // 2
    up = pltpu.roll(x, x.shape[-1] - half, 1)
    down = pltpu.roll(x, half, 1)
    return x * c + up * s_up + down * s_down


def _mod_kernel(c_ref, w_ref, b_ref, o_ref):
    ca = _silu(c_ref[...]).astype(BF16)
    o_ref[0] = _dot(ca, w_ref[0].astype(BF16)) + b_ref[0]


def _modulation(c, mod_w, mod_b):
    bsz = c.shape[0]
    tn = 1024
    n_out = mod_w.shape[-1]
    return pl.pallas_call(
        _mod_kernel,
        grid=(DEPTH, n_out // tn),
        in_specs=[
            pl.BlockSpec((bsz, D_MODEL), lambda l, j: (0, 0)),
            pl.BlockSpec((1, D_MODEL, tn), lambda l, j: (l, 0, j)),
            pl.BlockSpec((1, 1, tn), lambda l, j: (l, 0, j)),
        ],
        out_specs=pl.BlockSpec((1, bsz, tn), lambda l, j: (l, 0, j)),
        out_shape=jax.ShapeDtypeStruct((DEPTH, bsz, n_out), F32),
        name="modulation",
    )(c, mod_w, mod_b.reshape(DEPTH, 1, n_out))


def _even_in_kernel(x_ref, mod_ref, nw_ref, tab_ref,
                    wq_ref, wk_ref, wv_ref, wr_ref, wl_ref, gw_ref, gb_ref, wsq_ref, wskv_ref,
                    gq_ref, gk_ref, gv_ref, gr_ref, ga_ref, sq_ref, skv_ref):
    m = mod_ref[...]
    h = _modulated_norm(x_ref[...], nw_ref[...], m[:, D_MODEL:2 * D_MODEL], m[:, 0:D_MODEL]).astype(BF16)
    gq_ref[...] = _dot(h, wq_ref[...]) * (HEAD_DIM ** -0.5)
    gk_ref[...] = _dot(h, wk_ref[...])
    gv_ref[...] = _dot(h, wv_ref[...]).astype(BF16)
    gr_ref[...] = _silu(_dot(h, wr_ref[...]))
    glr = _dot(h, wl_ref[...]).astype(BF16)
    gate = _dot(glr, gw_ref[...]) + gb_ref[...]
    ga_ref[...] = _log_sigmoid(gate) / GLA_GATE_NORMALIZER
    tab = _rope_tables(tab_ref[...])
    sq = _dot(h, wsq_ref[...])
    sq_ref[...] = (_rope(sq, tab) * (LOG2_E * HEAD_DIM ** -0.5)).astype(BF16)
    skv = _dot(h, wskv_ref[...])
    sk = _rope(skv[:, :LANES], tab)
    skv_ref[...] = jnp.concatenate([sk, skv[:, LANES:]], axis=-1).astype(BF16)


def _odd_in_kernel(x_ref, mod_ref, nw_ref, tab_ref, llb_ref, l1m_ref, oml_ref,
                   wdq_ref, wdk_ref, wdv_ref, whq_ref, whf_ref, whi_ref, whg_ref,
                   dq_ref, dk_ref, dvt_ref, hq_ref, hk_ref, hf_ref, hi_ref, hg_ref):
    m = mod_ref[...]
    h = _modulated_norm(x_ref[...], nw_ref[...], m[:, D_MODEL:2 * D_MODEL], m[:, 0:D_MODEL]).astype(BF16)
    tab = _rope_tables(tab_ref[...])
    dq_ref[...] = (_rope(_dot(h, wdq_ref[...]), tab) * (LOG2_E * HEAD_DIM ** -0.5)).astype(BF16)
    dk_ref[...] = _rope(_dot(h, wdk_ref[...]), tab).astype(BF16)
    dv = _dot(h, wdv_ref[...])
    for t in range(dvt_ref.shape[0]):
        for hd in range(dvt_ref.shape[1]):
            tile = dv[t * TQ_DIFF:(t + 1) * TQ_DIFF, hd * LANES:(hd + 1) * LANES]
            dvt_ref[t, hd] = tile.T.astype(BF16)
    hq_ref[...] = _silu(_dot(h, whq_ref[...])) * (HGRN_EXPAND ** -0.5)
    z = _dot(h, whf_ref[...])
    a1 = llb_ref[...]
    a2 = l1m_ref[...] + _log_sigmoid(z)
    hf_ref[...] = jnp.maximum(a1, a2) + jnp.log(1.0 + jnp.exp(-jnp.abs(a1 - a2)))
    hk_ref[...] = oml_ref[...] * _sigmoid(-z)
    hi_ref[...] = _dot(h, whi_ref[...]).astype(BF16)
    hg_ref[...] = _silu(_dot(h, whg_ref[...]))


def _tok_spec(width):
    return pl.BlockSpec((None, TM_PROJ, width), lambda b, t: (b, t, 0))


def _const_spec(shape):
    nd = len(shape)
    return pl.BlockSpec(shape, lambda b, t: (0,) * nd)


def _in_projection(kern, name, x, mod_l, nw, table, consts, weights, outs):
    bsz, seq, _ = x.shape
    in_specs = ([_tok_spec(D_MODEL), pl.BlockSpec((None, 1, 6 * D_MODEL), lambda b, t: (b, 0, 0)),
                 _const_spec(nw.shape), _tok_spec(2 * LANES)]
                + [_const_spec(a.shape) for a in consts]
                + [_const_spec(w.shape) for w in weights])
    out_specs = [_tok_spec(o[0]) if len(o) == 2 else o[1] for o in outs]
    out_shape = [jax.ShapeDtypeStruct((bsz, seq, o[0]) if len(o) == 2 else o[0], o[-1]) for o in outs]
    return pl.pallas_call(
        kern,
        grid=(bsz, seq // TM_PROJ),
        in_specs=in_specs,
        out_specs=out_specs,
        out_shape=out_shape,
        compiler_params=pltpu.CompilerParams(
            dimension_semantics=("arbitrary", "arbitrary"), vmem_limit_bytes=VMEM_LIMIT),
        name=name,
    )(x, mod_l, nw, table, *consts, *weights)


def _gl_kernel(q_ref, k_ref, g_ref, v_ref, gate_ref, nw_ref, o_ref, st_ref, cum_ref, inter_ref, *, heads):
    @pl.when(pl.program_id(2) == 0)
    def _():
        st_ref[...] = jnp.zeros_like(st_ref)

    dk = LANES // heads
    row = lax.broadcasted_iota(jnp.int32, (CHUNK, CHUNK), 0)
    col = lax.broadcasted_iota(jnp.int32, (CHUNK, CHUNK), 1)
    causal = row >= col
    tri = jnp.where(causal, 1.0, 0.0).astype(BF16)
    lane_sub = lax.broadcasted_iota(jnp.int32, (SUB, LANES), 1)
    lane_chunk = lax.broadcasted_iota(jnp.int32, (CHUNK, LANES), 1)
    nw = nw_ref[...]
    n_sub = CHUNK // SUB
    zeros_sub = jnp.zeros((SUB, LANES), BF16)

    def own(t, lane, p):
        return t if heads == 1 else jnp.where((lane >= p * dk) & (lane < (p + 1) * dk), t, 0.0)

    chunks = [slice(c * CHUNK, (c + 1) * CHUNK) for c in range(q_ref.shape[0] // CHUNK)]
    heads_r = range(heads)

    cum = []
    spread = jnp.zeros((1, LANES), F32)
    for rows in chunks:
        g = g_ref[rows, :]
        for a in range(n_sub):
            spread = jnp.maximum(spread, -jnp.sum(g[SUB * a:SUB * (a + 1)], axis=0, keepdims=True))
        g1 = g.astype(BF16)
        r1 = g - g1.astype(F32)
        g2 = r1.astype(BF16)
        g3 = (r1 - g2.astype(F32)).astype(BF16)
        cum.append(_dot(tri, g1) + _dot(tri, g2) + _dot(tri, g3))

    q_in, decay_tot, k_own, k_stack, q_stack = [], [], [], [], []
    for rows, b in zip(chunks, cum):
        q = q_ref[rows, :]
        k = k_ref[rows, :]
        btot = b[CHUNK - 1:CHUNK, :]
        q_in.append((q * jnp.exp(b)).astype(BF16))
        k_out = k * jnp.exp(btot - b)
        k_own.append([own(k_out, lane_chunk, p).astype(BF16) for p in heads_r])
        decay_tot.append(jnp.exp(btot))
        k_groups, q_groups = [], []
        for a in range(n_sub):
            hi = SUB * (a + 1)
            lvl = b[SUB * a - 1:SUB * a, :] if a else jnp.zeros((1, LANES), F32)
            kk = (k[:hi] * jnp.exp(jnp.minimum(lvl - b[:hi], EXP_CLAMP))).astype(BF16)
            if hi < CHUNK:
                kk = jnp.concatenate([kk, jnp.zeros((CHUNK - hi, LANES), BF16)], axis=0)
            k_groups.append(kk)
            q_groups.append(q[SUB * a:hi] * jnp.exp(b[SUB * a:hi] - lvl))
        k_stack.append(jnp.concatenate(k_groups, axis=1))
        per_head = []
        for p in heads_r:
            q_rows = []
            for a in range(n_sub):
                qa = own(q_groups[a], lane_sub, p).astype(BF16)
                q_rows.append(jnp.concatenate([zeros_sub] * a + [qa] + [zeros_sub] * (n_sub - 1 - a), axis=1))
            per_head.append(jnp.concatenate(q_rows, axis=0))
        q_stack.append(per_head)

    vals = [[v_ref[rows, p * LANES:(p + 1) * LANES] for p in heads_r] for rows in chunks]
    scores = [[lax.dot_general(q_stack[c][p], k_stack[c], NT_DIMS, preferred_element_type=F32)
               for p in heads_r] for c in range(len(chunks))]
    incr = [[lax.dot_general(vals[c][p], k_own[c][p], TN_DIMS, preferred_element_type=F32)
             for p in heads_r] for c in range(len(chunks))]

    outs = []
    for p in heads_r:
        st = st_ref[p]
        for c in range(len(chunks)):
            s = jnp.where(causal, scores[c][p], 0.0).astype(BF16)
            inter = lax.dot_general(q_in[c], st.astype(BF16), NT_DIMS, preferred_element_type=F32)
            inter_ref[p, chunks[c], :] = inter
            o = _dot(s, vals[c][p]) + inter
            outs.append((c, p, o))
            st = st * decay_tot[c] + incr[c][p]
        st_ref[p] = st

    def finish(c, p, o):
        cols = slice(p * LANES, (p + 1) * LANES)
        o_ref[chunks[c], cols] = (_head_norm(o, nw) * gate_ref[chunks[c], cols]).astype(BF16)

    for c, p, o in outs:
        finish(c, p, o)

    for c, b in enumerate(cum):
        cum_ref[chunks[c], :] = b

    @pl.when(jnp.max(spread) > EXP_CLAMP)
    def _():
        for c, rows in enumerate(chunks):
            q = q_ref[rows, :]
            b = cum_ref[rows, :]

            def key_column(j, acc):
                kj = k_ref[pl.ds(c * CHUNK + j, 1), :]
                bj = cum_ref[pl.ds(c * CHUNK + j, 1), :]
                t = q * kj * jnp.exp(jnp.minimum(b - bj, 0.0))
                return tuple(jnp.where(col == j, jnp.sum(own(t, lane_chunk, p), axis=1, keepdims=True), acc[p])
                             for p in heads_r)

            exact = lax.fori_loop(0, CHUNK, key_column, tuple(jnp.zeros((CHUNK, CHUNK), F32) for _ in heads_r))
            for p in heads_r:
                s = jnp.where(causal, exact[p], 0.0).astype(BF16)
                finish(c, p, _dot(s, v_ref[rows, p * LANES:(p + 1) * LANES]) + inter_ref[p, rows, :])


def _gated_linear(q, k, g, v, gate, nw, heads):
    bsz, seq, width = q.shape
    groups = width // LANES
    vw = heads * LANES
    qspec = pl.BlockSpec((None, TS_GL, LANES), lambda b, gi, s: (b, s, gi))
    vspec = pl.BlockSpec((None, TS_GL, vw), lambda b, gi, s: (b, s, gi))
    return pl.pallas_call(
        functools.partial(_gl_kernel, heads=heads),
        grid=(bsz, groups, seq // TS_GL),
        in_specs=[qspec, qspec, qspec, vspec, vspec, pl.BlockSpec((1, LANES), lambda b, gi, s: (0, 0))],
        out_specs=vspec,
        out_shape=jax.ShapeDtypeStruct((bsz, seq, groups * vw), BF16),
        scratch_shapes=[pltpu.VMEM((heads, LANES, LANES), F32), pltpu.VMEM((TS_GL, LANES), F32),
                        pltpu.VMEM((heads, TS_GL, LANES), F32)],
        compiler_params=pltpu.CompilerParams(
            dimension_semantics=("arbitrary", "arbitrary", "arbitrary"), vmem_limit_bytes=VMEM_LIMIT),
        name=f"gated_linear_{heads}",
    )(q, k, g, v, gate, nw)


def _swa_kernel(sink_ref, q_ref, cur_ref, prev_ref, o_ref):
    n = pl.program_id(1)
    cur = cur_ref[...]
    prev = prev_ref[...]
    kk = jnp.concatenate([prev[:, :LANES], cur[:, :LANES]], axis=0)
    vv = jnp.concatenate([prev[:, LANES:], cur[:, LANES:]], axis=0)
    kj = lax.broadcasted_iota(jnp.int32, (2 * WINDOW, WINDOW), 0)
    qi = lax.broadcasted_iota(jnp.int32, (2 * WINDOW, WINDOW), 1) + WINDOW
    rel = qi - kj
    mask = (rel >= 0) & (rel < WINDOW) & ((n > 0) | (kj >= WINDOW))
    lane = lax.broadcasted_iota(jnp.int32, (WINDOW, LANES), 1)
    low = lane < HEAD_DIM
    per_kv = N_SWA_HEADS // N_SWA_KV_HEADS
    chains = [(i, kv) for i in range(per_kv) for kv in range(N_SWA_KV_HEADS)]
    scores = []
    for i, kv in chains:
        qg = q_ref[:, i * LANES:(i + 1) * LANES]
        qm = jnp.where(low if kv == 0 else ~low, qg, jnp.zeros_like(qg))
        scores.append(lax.dot_general(kk, qm, NT_DIMS, preferred_element_type=F32))
    probs, inv = [], []
    for (i, kv), s in zip(chains, scores):
        sink = sink_ref[kv * per_kv + i] * LOG2_E
        s = jnp.where(mask, s, -jnp.inf)
        m = jnp.maximum(jnp.max(s, axis=0, keepdims=True), sink)
        p = jnp.exp2(s - m)
        inv.append(1.0 / (jnp.sum(p, axis=0, keepdims=True) + jnp.exp2(sink - m)))
        probs.append(p.astype(BF16))
    outs = [lax.dot_general(vv, p, TN_DIMS, preferred_element_type=F32) * r for p, r in zip(probs, inv)]
    top = lax.broadcasted_iota(jnp.int32, (LANES, WINDOW), 0) < HEAD_DIM
    for i in range(per_kv):
        o_t = jnp.where(top, outs[2 * i], outs[2 * i + 1])
        o_ref[:, i * LANES:(i + 1) * LANES] = o_t.T.astype(BF16)


def _sliding_window(sinks, q, kv):
    bsz, seq, _ = q.shape
    return pl.pallas_call(
        _swa_kernel,
        grid=(bsz, seq // WINDOW),
        in_specs=[
            pl.BlockSpec(memory_space=pltpu.SMEM),
            pl.BlockSpec((None, WINDOW, MIX_HALF), lambda b, n: (b, n, 0)),
            pl.BlockSpec((None, WINDOW, 2 * LANES), lambda b, n: (b, n, 0)),
            pl.BlockSpec((None, WINDOW, 2 * LANES), lambda b, n: (b, jnp.maximum(n - 1, 0), 0)),
        ],
        out_specs=pl.BlockSpec((None, WINDOW, MIX_HALF), lambda b, n: (b, n, 0)),
        out_shape=jax.ShapeDtypeStruct((bsz, seq, MIX_HALF), BF16),
        compiler_params=pltpu.CompilerParams(dimension_semantics=("arbitrary", "arbitrary")),
        name="sliding_window",
    )(sinks, q, kv, kv)


def _diff_kernel(lam_ref, q_ref, k_ref, vt_ref, nw_ref, o_ref, acc_ref, stat_ref, *, out_scale):
    qi = pl.program_id(1)
    tq = q_ref.shape[0]
    nh = q_ref.shape[1] // LANES
    lane = lax.broadcasted_iota(jnp.int32, (tq, LANES), 1)
    qs = []
    for h in range(nh):
        q = q_ref[:, h * LANES:(h + 1) * LANES]
        zero = jnp.zeros_like(q)
        qs += [jnp.where(lane < HEAD_DIM, q, zero), jnp.where(lane >= HEAD_DIM, q, zero)]
    key = lax.broadcasted_iota(jnp.int32, (tq, tq), 0)
    qry = lax.broadcasted_iota(jnp.int32, (tq, tq), 1)
    causal = key <= qry
    acc_ref[...] = jnp.zeros_like(acc_ref)
    for c in range(2 * nh):
        stat_ref[2 * c:2 * c + 1, :] = jnp.full((1, tq), -jnp.inf, F32)
        stat_ref[2 * c + 1:2 * c + 2, :] = jnp.zeros((1, tq), F32)

    def update(j, masked):
        rows = pl.ds(pl.multiple_of(j * tq, tq), tq)
        scores = [lax.dot_general(k_ref[rows, (c // 2) * LANES:(c // 2 + 1) * LANES], qs[c], NT_DIMS,
                                  preferred_element_type=F32) for c in range(2 * nh)]
        probs, alphas = [], []
        for c, s in enumerate(scores):
            if masked:
                s = jnp.where(causal, s, -jnp.inf)
            m = stat_ref[2 * c:2 * c + 1, :]
            m_new = jnp.maximum(m, jnp.max(s, axis=0, keepdims=True))
            alpha = jnp.exp2(m - m_new)
            p = jnp.exp2(s - m_new)
            stat_ref[2 * c:2 * c + 1, :] = m_new
            stat_ref[2 * c + 1:2 * c + 2, :] = (alpha * stat_ref[2 * c + 1:2 * c + 2, :]
                                                + jnp.sum(p, axis=0, keepdims=True))
            probs.append(p.astype(BF16))
            alphas.append(alpha)
        for c in range(2 * nh):
            acc_ref[c] = alphas[c] * acc_ref[c] + _dot(vt_ref[j, c // 2], probs[c])

    def body(j, carry):
        update(j, False)
        return carry

    lax.fori_loop(0, qi, body, 0)
    update(qi, True)
    lam = lam_ref[0]
    nw = nw_ref[...]
    for h in range(nh):
        l1 = stat_ref[4 * h + 1:4 * h + 2, :]
        l2 = stat_ref[4 * h + 3:4 * h + 4, :]
        o_t = acc_ref[2 * h] / l1 - lam * (acc_ref[2 * h + 1] / l2)
        o_ref[:, h * LANES:(h + 1) * LANES] = (_head_norm(o_t.T, nw) * out_scale).astype(BF16)


def _differential(lam, q, k, vt, nw, out_scale):
    bsz, seq, width = q.shape
    nh = width // LANES
    nt = seq // TQ_DIFF
    q_spec = pl.BlockSpec((None, TQ_DIFF, width), lambda b, i: (b, i, 0))
    return pl.pallas_call(
        functools.partial(_diff_kernel, out_scale=out_scale),
        grid=(bsz, nt),
        in_specs=[pl.BlockSpec(memory_space=pltpu.SMEM), q_spec,
                  pl.BlockSpec((None, seq, width), lambda b, i: (b, 0, 0)),
                  pl.BlockSpec((None, nt, nh, LANES, TQ_DIFF), lambda b, i: (b, 0, 0, 0, 0)),
                  pl.BlockSpec((1, LANES), lambda b, i: (0, 0))],
        out_specs=q_spec,
        out_shape=jax.ShapeDtypeStruct((bsz, seq, width), BF16),
        scratch_shapes=[pltpu.VMEM((2 * nh, LANES, TQ_DIFF), F32), pltpu.VMEM((4 * nh, TQ_DIFF), F32)],
        compiler_params=pltpu.CompilerParams(
            dimension_semantics=("arbitrary", "arbitrary"), vmem_limit_bytes=VMEM_LIMIT),
        name="differential",
    )(lam, q, k, vt, nw)


def _out_ffn_kernel(x_ref, oa_ref, ob_ref, mod_ref, nw_ref, fw_ref, woa_ref, wob_ref,
                    wi_ref, cw_ref, cb_ref, wo_ref,
                    y_ref, act_ref, carry_ref, *, final):
    t = pl.program_id(1)
    m = mod_ref[...]
    d = D_MODEL
    g1, sh2, sc2, g2 = m[:, 2 * d:3 * d], m[:, 3 * d:4 * d], m[:, 4 * d:5 * d], m[:, 5 * d:6 * d]
    mix = _dot(oa_ref[...], woa_ref[...]) + _dot(ob_ref[...], wob_ref[...])
    x1 = x_ref[...] + g1 * mix
    h = _modulated_norm(x1, nw_ref[...], sc2, sh2).astype(BF16)

    @pl.when(t == 0)
    def _():
        carry_ref[...] = jnp.zeros_like(carry_ref)

    tm = x1.shape[0]
    row = lax.broadcasted_iota(jnp.int32, (tm, FF_CHUNK), 0)
    for c in range(D_FF // FF_CHUNK):
        cols = slice(c * FF_CHUNK, (c + 1) * FF_CHUNK)
        a = _dot(h, wi_ref[:, cols])
        u = _dot(h, wi_ref[:, D_FF + c * FF_CHUNK:D_FF + (c + 1) * FF_CHUNK])
        tail = carry_ref[:, cols]
        cw = cw_ref[:, cols]
        prev1 = jnp.where(row < 1, tail[7:8, :], pltpu.roll(a, 1, 0))
        prev2 = jnp.where(row < 1, tail[6:7, :], jnp.where(row < 2, tail[7:8, :], pltpu.roll(a, 2, 0)))
        carry_ref[:, cols] = a[tm - 8:tm, :]
        conv = prev2 * cw[0:1, :] + prev1 * cw[1:2, :] + a * cw[2:3, :] + cb_ref[:, cols]
        act_ref[:, cols] = (_silu(conv) * u).astype(BF16)
    y = x1 + g2 * _dot(act_ref[...], wo_ref[...])
    if final:
        ms = jnp.mean(y * y, axis=-1, keepdims=True)
        y = (y * lax.rsqrt(ms + NORM_EPS)) * fw_ref[...]
    y_ref[...] = y


def _out_ffn(x, oa, ob, mod_l, nw, fw, woa, wob, wi, cw, cb, wo, final):
    bsz, seq, _ = x.shape

    def resident(shape):
        nd = len(shape)
        return pl.BlockSpec(shape, lambda b, t: (0,) * nd, pipeline_mode=pl.Buffered(1))

    return pl.pallas_call(
        functools.partial(_out_ffn_kernel, final=final),
        grid=(bsz, seq // TM_PROJ),
        in_specs=[_tok_spec(D_MODEL), _tok_spec(MIX_HALF), _tok_spec(MIX_HALF),
                  pl.BlockSpec((None, 1, 6 * D_MODEL), lambda b, t: (b, 0, 0)),
                  _const_spec(nw.shape), _const_spec(fw.shape),
                  resident(woa.shape), resident(wob.shape), resident(wi.shape),
                  _const_spec(cw.shape), _const_spec(cb.shape), resident(wo.shape)],
        out_specs=_tok_spec(D_MODEL),
        out_shape=jax.ShapeDtypeStruct(x.shape, F32),
        scratch_shapes=[pltpu.VMEM((TM_PROJ, D_FF), BF16), pltpu.VMEM((8, D_FF), F32)],
        compiler_params=pltpu.CompilerParams(
            dimension_semantics=("arbitrary", "arbitrary"), vmem_limit_bytes=VMEM_LIMIT),
        name="out_ffn",
    )(x, oa, ob, mod_l, nw, fw, woa, wob, wi, cw, cb, wo)


def _rope_lane_table(positions):
    half = ROT_DIMS // 2
    inv_freq = ROPE_THETA ** (-jnp.arange(0, ROT_DIMS, 2, dtype=F32) / ROT_DIMS)
    ang = inv_freq[:, None] * positions.astype(F32).reshape(1, -1)
    cs = jnp.concatenate([jnp.cos(ang), jnp.sin(ang)], axis=0)
    d = np.arange(LANES) % HEAD_DIM
    sel = np.zeros((2 * half, 2 * LANES), np.float32)
    base = np.zeros((1, 2 * LANES), np.float32)
    for lane in range(LANES):
        if d[lane] < ROT_DIMS:
            sel[d[lane] % half, lane] = 1.0
            sel[half + d[lane] % half, LANES + lane] = -1.0 if d[lane] < half else 1.0
        else:
            base[0, lane] = 1.0
    tab = lax.dot_general(cs, jnp.asarray(sel), TN_DIMS, precision=lax.Precision.HIGHEST) + base
    return tab.reshape(positions.shape + (2 * LANES,))


def _cols(w, start, width):
    return w[:, start:start + width].astype(BF16)


SWA_HEAD_ORDER = (0, 4, 1, 5, 2, 6, 3, 7)


def kernel(x, c, positions, mod_w, mod_b, norm_mix_w, norm_ffn_w, ev_w_in, gla_gate_w, gla_gate_b,
           gla_norm_w, swa_sinks, ev_w_out, od_w_in, diff_lambda, diff_norm_w, hgrn_lb_logits,
           hgrn_norm_w, od_w_out, ffn_w_in, ffn_conv_w, ffn_conv_b, ffn_w_out, final_norm_w):
    bsz, seq, _ = x.shape
    table = _rope_lane_table(positions)
    lbs = jnp.cumsum(jax.nn.softmax(hgrn_lb_logits.astype(F32), axis=0), axis=0)
    lbs = lbs - lbs[0:1]
    mod = _modulation(c, mod_w, mod_b).reshape(DEPTH, bsz, 1, 6 * D_MODEL)
    swa_cols = jnp.concatenate([jnp.arange(HEAD_DIM) + HEAD_DIM * h for h in SWA_HEAD_ORDER])
    fw = final_norm_w.reshape(1, D_MODEL)

    for l in range(DEPTH):
        j = l // 2
        nw_mix = norm_mix_w[l].reshape(1, D_MODEL)
        if l % 2 == 0:
            w = ev_w_in[j]
            w_sq = w[:, 1552:2064][:, swa_cols].astype(BF16)
            w_glr = jnp.pad(w[:, 1536:1552], ((0, 0), (0, LANES - GLA_GATE_RANK))).astype(BF16)
            gate_w = jnp.pad(gla_gate_w[j], ((0, LANES - GLA_GATE_RANK), (0, 0))).astype(BF16)
            weights = [_cols(w, 0, 256), _cols(w, 256, 256), _cols(w, 512, 512), _cols(w, 1024, 512),
                       w_glr, gate_w, gla_gate_b[j].reshape(1, -1), w_sq, _cols(w, 2064, 256)]
            outs = [(256, F32), (256, F32), (512, BF16), (512, F32), (256, F32), (512, BF16), (256, BF16)]
            gq, gk, gv, gr, ga, sq, skv = _in_projection(
                _even_in_kernel, "even_in", x, mod[l], nw_mix, table, [], weights, outs)
            oa = _gated_linear(gq, gk, ga, gv, gr, gla_norm_w[j].reshape(1, LANES), heads=2)
            ob = _sliding_window(swa_sinks[j], sq, skv)
            w_out = ev_w_out[j]
            wob = w_out[MIX_HALF:][swa_cols].astype(BF16)
        else:
            w = od_w_in[j]
            lb = lbs[j].reshape(1, -1)
            consts = [jnp.log(lb), jnp.log1p(-lb), 1.0 - lb]
            weights = [_cols(w, 512 * i, 512) for i in range(7)]
            tiles = TM_PROJ // TQ_DIFF
            n_dh = MIX_HALF // LANES
            vt_out = ((bsz, seq // TQ_DIFF, n_dh, LANES, TQ_DIFF),
                      pl.BlockSpec((None, tiles, n_dh, LANES, TQ_DIFF), lambda b, t: (b, t, 0, 0, 0)), BF16)
            outs = [(512, BF16), (512, BF16), vt_out, (512, F32), (512, F32), (512, F32),
                    (512, BF16), (512, F32)]
            dq, dk, dvt, hq, hk, hf, hi, hg = _in_projection(
                _odd_in_kernel, "odd_in", x, mod[l], nw_mix, table, consts, weights, outs)
            lam_init = 0.8 - 0.6 * math.exp(-0.3 * l)
            lv = diff_lambda[j].astype(F32)
            lam = jnp.exp(jnp.sum(lv[0] * lv[1])) - jnp.exp(jnp.sum(lv[2] * lv[3])) + lam_init
            oa = _differential(lam.reshape(1), dq, dk, dvt, diff_norm_w[j].reshape(1, LANES), 1.0 - lam_init)
            ob = _gated_linear(hq, hk, hf, hi, hg, hgrn_norm_w[j].reshape(1, LANES), heads=1)
            w_out = od_w_out[j]
            wob = w_out[MIX_HALF:].astype(BF16)
        woa = w_out[:MIX_HALF].astype(BF16)
        x = _out_ffn(x, oa, ob, mod[l], norm_ffn_w[l].reshape(1, D_MODEL), fw, woa, wob,
                     ffn_w_in[l].astype(BF16), ffn_conv_w[l], ffn_conv_b[l].reshape(1, D_FF),
                     ffn_w_out[l].astype(BF16), final=(l == DEPTH - 1))
    return x
```

```python
import functools
import math

import jax
import numpy as np
import jax.numpy as jnp
from jax import lax
from jax.experimental import pallas as pl
from jax.experimental.pallas import tpu as pltpu

F32 = jnp.float32
BF16 = jnp.bfloat16

D_MODEL = 1024
DEPTH = 4
HEAD_DIM = 64
ROT_DIMS = HEAD_DIM // 4
ROPE_THETA = 500000.0
GLA_GATE_RANK = 16
GLA_GATE_NORMALIZER = 16.0
N_SWA_HEADS = 8
N_SWA_KV_HEADS = 2
WINDOW = 128
HGRN_EXPAND = 128
D_FF = 2816
MIX_HALF = D_MODEL // 2
NORM_EPS = 1e-6
LOG2_E = math.log2(math.e)

LANES = 128
CHUNK = 64
SUB = 32
EXP_CLAMP = 60.0
TM_PROJ = 512
TS_GL = 512
TQ_DIFF = 256
SWA_BLOCKS = 4
ONES_ROWS = 16
FF_CHUNK = 256
VMEM_LIMIT = 56 * 1024 * 1024

NT_DIMS = (((1,), (1,)), ((), ()))
TN_DIMS = (((0,), (0,)), ((), ()))


def _sigmoid(z):
    return 1.0 / (1.0 + jnp.exp(-z))


def _silu(z):
    return z * _sigmoid(z)


def _log_sigmoid(z):
    return jnp.minimum(z, 0.0) - jnp.log(1.0 + jnp.exp(-jnp.abs(z)))


def _dot(a, b):
    return jnp.dot(a, b, preferred_element_type=F32)


def _modulated_norm(x, nw, sc, sh):
    ms = jnp.mean(x * x, axis=-1, keepdims=True)
    return ((x * lax.rsqrt(ms + NORM_EPS)) * nw) * (1.0 + sc) + sh


def _head_norm(o, nw):
    ms = jnp.mean(o * o, axis=-1, keepdims=True)
    return (o * lax.rsqrt(ms + NORM_EPS)) * nw


def _rope_tables(tab):
    c, s = tab[:, :LANES], tab[:, LANES:]
    lane = lax.broadcasted_iota(jnp.int32, c.shape, 1)
    first_half = (lane & (HEAD_DIM - 1)) < ROT_DIMS // 2
    return c, jnp.where(first_half, s, 0.0), jnp.where(first_half, 0.0, s)


def _rope(x, tables):
    n = x.shape[-1] // LANES
    c, s_up, s_down = (jnp.concatenate([t] * n, axis=-1) if n > 1 else t for t in tables)
    half = ROT_DIMS // 2
    up = pltpu.roll(x, x.shape[-1] - half, 1)
    down = pltpu.roll(x, half, 1)
    return x * c + up * s_up + down * s_down


def _mod_kernel(c_ref, w_ref, b_ref, o_ref):
    ca = _silu(c_ref[...]).astype(BF16)
    o_ref[0] = _dot(ca, w_ref[0].astype(BF16)) + b_ref[0]


def _modulation(c, mod_w, mod_b):
    bsz = c.shape[0]
    tn = 1024
    n_out = mod_w.shape[-1]
    return pl.pallas_call(
        _mod_kernel,
        grid=(DEPTH, n_out // tn),
        in_specs=[
            pl.BlockSpec((bsz, D_MODEL), lambda l, j: (0, 0)),
            pl.BlockSpec((1, D_MODEL, tn), lambda l, j: (l, 0, j)),
            pl.BlockSpec((1, 1, tn), lambda l, j: (l, 0, j)),
        ],
        out_specs=pl.BlockSpec((1, bsz, tn), lambda l, j: (l, 0, j)),
        out_shape=jax.ShapeDtypeStruct((DEPTH, bsz, n_out), F32),
        name="modulation",
    )(c, mod_w, mod_b.reshape(DEPTH, 1, n_out))


def _even_in_kernel(x_ref, mod_ref, nw_ref, tab_ref,
                    wg_ref, wl_ref, gw_ref, gb_ref, wsq_ref, wskv_ref,
                    gq_ref, gk_ref, gv_ref, gr_ref, ga_ref, sq_ref, skv_ref):
    m = mod_ref[...]
    h = _modulated_norm(x_ref[...], nw_ref[...], m[:, D_MODEL:2 * D_MODEL], m[:, 0:D_MODEL]).astype(BF16)
    gq_ref[...] = _dot(h, wg_ref[:, 0:256]) * (HEAD_DIM ** -0.5)
    gk_ref[...] = _dot(h, wg_ref[:, 256:512])
    gv_ref[...] = _dot(h, wg_ref[:, 512:1024]).astype(BF16)
    gr_ref[...] = _silu(_dot(h, wg_ref[:, 1024:1536]))
    glr = _dot(h, wl_ref[...]).astype(BF16)
    gate = _dot(glr, gw_ref[...]) + gb_ref[...]
    ga_ref[...] = _log_sigmoid(gate) / GLA_GATE_NORMALIZER
    tab = _rope_tables(tab_ref[...])
    sq = _dot(h, wsq_ref[...])
    sq_ref[...] = (_rope(sq, tab) * (LOG2_E * HEAD_DIM ** -0.5)).astype(BF16)
    skv = _dot(h, wskv_ref[...])
    sk = _rope(skv[:, :LANES], tab)
    skv_ref[...] = jnp.concatenate([sk, skv[:, LANES:]], axis=-1).astype(BF16)


def _odd_in_kernel(x_ref, mod_ref, nw_ref, tab_ref, llb_ref, l1m_ref, oml_ref, w_ref,
                   dq_ref, dk_ref, dvt_ref, hq_ref, hk_ref, hf_ref, hi_ref, hg_ref):
    wdq_ref, wdk_ref, wdv_ref, whq_ref, whf_ref, whi_ref, whg_ref = (
        w_ref.at[:, i * MIX_HALF:(i + 1) * MIX_HALF] for i in range(7))
    m = mod_ref[...]
    h = _modulated_norm(x_ref[...], nw_ref[...], m[:, D_MODEL:2 * D_MODEL], m[:, 0:D_MODEL]).astype(BF16)
    tab = _rope_tables(tab_ref[...])
    dq_ref[...] = (_rope(_dot(h, wdq_ref[...]), tab) * (LOG2_E * HEAD_DIM ** -0.5)).astype(BF16)
    dk_ref[...] = _rope(_dot(h, wdk_ref[...]), tab).astype(BF16)
    dv = _dot(h, wdv_ref[...])
    for t in range(dvt_ref.shape[0]):
        for hd in range(dvt_ref.shape[1]):
            tile = dv[t * TQ_DIFF:(t + 1) * TQ_DIFF, hd * LANES:(hd + 1) * LANES]
            dvt_ref[t, hd] = tile.T.astype(BF16)
    hq_ref[...] = _silu(_dot(h, whq_ref[...])) * (HGRN_EXPAND ** -0.5)
    z = _dot(h, whf_ref[...])
    a1 = llb_ref[...]
    a2 = l1m_ref[...] + _log_sigmoid(z)
    hf_ref[...] = jnp.maximum(a1, a2) + jnp.log(1.0 + jnp.exp(-jnp.abs(a1 - a2)))
    hk_ref[...] = oml_ref[...] * _sigmoid(-z)
    hi_ref[...] = _dot(h, whi_ref[...]).astype(BF16)
    hg_ref[...] = _silu(_dot(h, whg_ref[...]))


def _tok_spec(width):
    return pl.BlockSpec((None, TM_PROJ, width), lambda b, t: (b, t, 0))


def _const_spec(shape):
    nd = len(shape)
    return pl.BlockSpec(shape, lambda b, t: (0,) * nd)


def _in_projection(kern, name, x, mod_l, nw, table, consts, weights, outs):
    bsz, seq, _ = x.shape
    in_specs = ([_tok_spec(D_MODEL), pl.BlockSpec((None, 1, 6 * D_MODEL), lambda b, t: (b, 0, 0)),
                 _const_spec(nw.shape), _tok_spec(2 * LANES)]
                + [_const_spec(a.shape) for a in consts]
                + [_const_spec(w.shape) for w in weights])
    out_specs = [_tok_spec(o[0]) if len(o) == 2 else o[1] for o in outs]
    out_shape = [jax.ShapeDtypeStruct((bsz, seq, o[0]) if len(o) == 2 else o[0], o[-1]) for o in outs]
    return pl.pallas_call(
        kern,
        grid=(bsz, seq // TM_PROJ),
        in_specs=in_specs,
        out_specs=out_specs,
        out_shape=out_shape,
        compiler_params=pltpu.CompilerParams(
            dimension_semantics=("arbitrary", "arbitrary"), vmem_limit_bytes=VMEM_LIMIT),
        name=name,
    )(x, mod_l, nw, table, *consts, *weights)


def _gl_kernel(q_ref, k_ref, g_ref, v_ref, gate_ref, nw_ref, o_ref, st_ref, cum_ref, inter_ref, *, heads):
    @pl.when(pl.program_id(2) == 0)
    def _():
        st_ref[...] = jnp.zeros_like(st_ref)

    dk = LANES // heads
    row = lax.broadcasted_iota(jnp.int32, (CHUNK, CHUNK), 0)
    col = lax.broadcasted_iota(jnp.int32, (CHUNK, CHUNK), 1)
    causal = row >= col
    tri = jnp.where(causal, 1.0, 0.0).astype(BF16)
    lane_sub = lax.broadcasted_iota(jnp.int32, (SUB, LANES), 1)
    lane_chunk = lax.broadcasted_iota(jnp.int32, (CHUNK, LANES), 1)
    nw = nw_ref[...]
    n_sub = CHUNK // SUB
    zeros_sub = jnp.zeros((SUB, LANES), BF16)

    def own(t, lane, p):
        return t if heads == 1 else jnp.where((lane >= p * dk) & (lane < (p + 1) * dk), t, 0.0)

    chunks = [slice(c * CHUNK, (c + 1) * CHUNK) for c in range(q_ref.shape[0] // CHUNK)]
    heads_r = range(heads)

    cum = []
    spread = jnp.zeros((1, LANES), F32)
    for rows in chunks:
        g = g_ref[rows, :]
        for a in range(n_sub):
            spread = jnp.maximum(spread, -jnp.sum(g[SUB * a:SUB * (a + 1)], axis=0, keepdims=True))
        g1 = g.astype(BF16)
        r1 = g - g1.astype(F32)
        g2 = r1.astype(BF16)
        g3 = (r1 - g2.astype(F32)).astype(BF16)
        cum.append(_dot(tri, g1) + _dot(tri, g2) + _dot(tri, g3))

    q_in, decay_tot, k_own, k_stack, q_stack = [], [], [], [], []
    for rows, b in zip(chunks, cum):
        q = q_ref[rows, :]
        k = k_ref[rows, :]
        btot = b[CHUNK - 1:CHUNK, :]
        q_in.append((q * jnp.exp(b)).astype(BF16))
        k_out = k * jnp.exp(btot - b)
        k_own.append([own(k_out, lane_chunk, p).astype(BF16) for p in heads_r])
        decay_tot.append(jnp.exp(btot))
        k_groups, q_groups = [], []
        for a in range(n_sub):
            hi = SUB * (a + 1)
            lvl = b[SUB * a - 1:SUB * a, :] if a else jnp.zeros((1, LANES), F32)
            kk = (k[:hi] * jnp.exp(jnp.minimum(lvl - b[:hi], EXP_CLAMP))).astype(BF16)
            if hi < CHUNK:
                kk = jnp.concatenate([kk, jnp.zeros((CHUNK - hi, LANES), BF16)], axis=0)
            k_groups.append(kk)
            q_groups.append(q[SUB * a:hi] * jnp.exp(b[SUB * a:hi] - lvl))
        k_stack.append(jnp.concatenate(k_groups, axis=1))
        per_head = []
        for p in heads_r:
            q_rows = []
            for a in range(n_sub):
                qa = own(q_groups[a], lane_sub, p).astype(BF16)
                q_rows.append(jnp.concatenate([zeros_sub] * a + [qa] + [zeros_sub] * (n_sub - 1 - a), axis=1))
            per_head.append(jnp.concatenate(q_rows, axis=0))
        q_stack.append(per_head)

    vals = [[v_ref[rows, p * LANES:(p + 1) * LANES] for p in heads_r] for rows in chunks]
    scores = [[lax.dot_general(q_stack[c][p], k_stack[c], NT_DIMS, preferred_element_type=F32)
               for p in heads_r] for c in range(len(chunks))]
    incr = [[lax.dot_general(vals[c][p], k_own[c][p], TN_DIMS, preferred_element_type=F32)
             for p in heads_r] for c in range(len(chunks))]

    outs = []
    for p in heads_r:
        st = st_ref[p]
        for c in range(len(chunks)):
            s = jnp.where(causal, scores[c][p], 0.0).astype(BF16)
            inter = lax.dot_general(q_in[c], st.astype(BF16), NT_DIMS, preferred_element_type=F32)
            inter_ref[p, chunks[c], :] = inter
            o = _dot(s, vals[c][p]) + inter
            outs.append((c, p, o))
            st = st * decay_tot[c] + incr[c][p]
        st_ref[p] = st

    def finish(c, p, o):
        cols = slice(p * LANES, (p + 1) * LANES)
        o_ref[chunks[c], cols] = (_head_norm(o, nw) * gate_ref[chunks[c], cols]).astype(BF16)

    for c, p, o in outs:
        finish(c, p, o)

    for c, b in enumerate(cum):
        cum_ref[chunks[c], :] = b

    @pl.when(jnp.max(spread) > EXP_CLAMP)
    def _():
        for c, rows in enumerate(chunks):
            q = q_ref[rows, :]
            b = cum_ref[rows, :]

            def key_column(j, acc):
                kj = k_ref[pl.ds(c * CHUNK + j, 1), :]
                bj = cum_ref[pl.ds(c * CHUNK + j, 1), :]
                t = q * kj * jnp.exp(jnp.minimum(b - bj, 0.0))
                return tuple(jnp.where(col == j, jnp.sum(own(t, lane_chunk, p), axis=1, keepdims=True), acc[p])
                             for p in heads_r)

            exact = lax.fori_loop(0, CHUNK, key_column, tuple(jnp.zeros((CHUNK, CHUNK), F32) for _ in heads_r))
            for p in heads_r:
                s = jnp.where(causal, exact[p], 0.0).astype(BF16)
                finish(c, p, _dot(s, v_ref[rows, p * LANES:(p + 1) * LANES]) + inter_ref[p, rows, :])


def _gated_linear(q, k, g, v, gate, nw, heads):
    bsz, seq, width = q.shape
    groups = width // LANES
    vw = heads * LANES
    qspec = pl.BlockSpec((None, TS_GL, LANES), lambda b, gi, s: (b, s, gi))
    vspec = pl.BlockSpec((None, TS_GL, vw), lambda b, gi, s: (b, s, gi))
    return pl.pallas_call(
        functools.partial(_gl_kernel, heads=heads),
        grid=(bsz, groups, seq // TS_GL),
        in_specs=[qspec, qspec, qspec, vspec, vspec, pl.BlockSpec((1, LANES), lambda b, gi, s: (0, 0))],
        out_specs=vspec,
        out_shape=jax.ShapeDtypeStruct((bsz, seq, groups * vw), BF16),
        scratch_shapes=[pltpu.VMEM((heads, LANES, LANES), F32), pltpu.VMEM((TS_GL, LANES), F32),
                        pltpu.VMEM((heads, TS_GL, LANES), F32)],
        compiler_params=pltpu.CompilerParams(
            dimension_semantics=("arbitrary", "arbitrary", "arbitrary"), vmem_limit_bytes=VMEM_LIMIT),
        name=f"gated_linear_{heads}",
    )(q, k, g, v, gate, nw)


def _swa_kernel(sink_ref, q_ref, cur_ref, prev_ref, o_ref):
    n = pl.program_id(1)
    kj = lax.broadcasted_iota(jnp.int32, (2 * WINDOW, WINDOW), 0)
    qi = lax.broadcasted_iota(jnp.int32, (2 * WINDOW, WINDOW), 1) + WINDOW
    rel = qi - kj
    band = (rel >= 0) & (rel < WINDOW)
    band_first = band & ((n > 0) | (kj >= WINDOW))
    lane = lax.broadcasted_iota(jnp.int32, (WINDOW, LANES), 1)
    low = lane < HEAD_DIM
    per_kv = N_SWA_HEADS // N_SWA_KV_HEADS
    chains = [(t, i, kv) for t in range(q_ref.shape[0] // WINDOW)
              for i in range(per_kv) for kv in range(N_SWA_KV_HEADS)]
    values, scores = {}, []
    for t, i, kv in chains:
        rows = slice(t * WINDOW, (t + 1) * WINDOW)
        if (i, kv) == (0, 0):
            cur = cur_ref[rows, :]
            prev = cur_ref[(t - 1) * WINDOW:t * WINDOW, :] if t else prev_ref[...]
            keys = jnp.concatenate([prev[:, :LANES], cur[:, :LANES]], axis=0)
            values[t] = jnp.concatenate([prev[:, LANES:], cur[:, LANES:]], axis=0)
        qg = q_ref[rows, i * LANES:(i + 1) * LANES]
        qm = jnp.where(low if kv == 0 else ~low, qg, jnp.zeros_like(qg))
        scores.append(lax.dot_general(keys, qm, NT_DIMS, preferred_element_type=F32))
    probs, inv = [], []
    for (t, i, kv), s in zip(chains, scores):
        sink = sink_ref[kv * per_kv + i] * LOG2_E
        s = jnp.where(band if t else band_first, s, -jnp.inf)
        m = jnp.maximum(jnp.max(s, axis=0, keepdims=True), sink)
        p = jnp.exp2(s - m)
        inv.append(1.0 / (jnp.sum(p, axis=0, keepdims=True) + jnp.exp2(sink - m)))
        probs.append(p.astype(BF16))
    outs = [lax.dot_general(values[t], p, TN_DIMS, preferred_element_type=F32) * r
            for (t, _, _), p, r in zip(chains, probs, inv)]
    top = lax.broadcasted_iota(jnp.int32, (LANES, WINDOW), 0) < HEAD_DIM
    for idx in range(0, len(chains), N_SWA_KV_HEADS):
        t, i, _ = chains[idx]
        o_t = jnp.where(top, outs[idx], outs[idx + 1])
        o_ref[t * WINDOW:(t + 1) * WINDOW, i * LANES:(i + 1) * LANES] = o_t.T.astype(BF16)


def _sliding_window(sinks, q, kv):
    bsz, seq, _ = q.shape
    tq = SWA_BLOCKS * WINDOW
    return pl.pallas_call(
        _swa_kernel,
        grid=(bsz, seq // tq),
        in_specs=[
            pl.BlockSpec(memory_space=pltpu.SMEM),
            pl.BlockSpec((None, tq, MIX_HALF), lambda b, n: (b, n, 0)),
            pl.BlockSpec((None, tq, 2 * LANES), lambda b, n: (b, n, 0)),
            pl.BlockSpec((None, WINDOW, 2 * LANES), lambda b, n: (b, jnp.maximum(SWA_BLOCKS * n - 1, 0), 0)),
        ],
        out_specs=pl.BlockSpec((None, tq, MIX_HALF), lambda b, n: (b, n, 0)),
        out_shape=jax.ShapeDtypeStruct((bsz, seq, MIX_HALF), BF16),
        compiler_params=pltpu.CompilerParams(
            dimension_semantics=("arbitrary", "arbitrary"), vmem_limit_bytes=VMEM_LIMIT),
        name="sliding_window",
    )(sinks, q, kv, kv)


def _diff_kernel(lam_ref, q_ref, k_ref, vt_ref, nw_ref, o_ref, acc_ref, max_ref, s_ref, *, out_scale):
    qi = pl.program_id(1)
    tq = q_ref.shape[0]
    nh = q_ref.shape[1] // LANES
    lane = lax.broadcasted_iota(jnp.int32, (tq, LANES), 1)
    qs = []
    for h in range(nh):
        q = q_ref[:, h * LANES:(h + 1) * LANES]
        zero = jnp.zeros_like(q)
        qs += [jnp.where(lane < HEAD_DIM, q, zero), jnp.where(lane >= HEAD_DIM, q, zero)]
    key = lax.broadcasted_iota(jnp.int32, (tq, tq), 0)
    qry = lax.broadcasted_iota(jnp.int32, (tq, tq), 1)
    causal = key <= qry
    ones = jnp.ones((acc_ref.shape[1] - LANES, tq), BF16)
    acc_ref[...] = jnp.zeros_like(acc_ref)
    max_ref[...] = jnp.full(max_ref.shape, -jnp.inf, F32)

    def issue_scores(j, c):
        rows = pl.ds(pl.multiple_of(j * tq, tq), tq)
        kt = k_ref[rows, (c // 2) * LANES:(c // 2 + 1) * LANES]
        s_ref[c] = lax.dot_general(kt, qs[c], NT_DIMS, preferred_element_type=F32)

    def consume(j, c, masked):
        s = s_ref[c]
        if masked:
            s = jnp.where(causal, s, -jnp.inf)
        m = max_ref[c:c + 1, :]
        m_new = jnp.maximum(m, jnp.max(s, axis=0, keepdims=True))
        max_ref[c:c + 1, :] = m_new
        p = jnp.exp2(s - m_new).astype(BF16)
        vt1 = jnp.concatenate([vt_ref[j, c // 2], ones], axis=0)
        acc_ref[c] = jnp.exp2(m - m_new) * acc_ref[c] + _dot(vt1, p)

    for c in range(2 * nh):
        issue_scores(0, c)

    def body(j, carry):
        for c in range(2 * nh):
            consume(j, c, False)
            issue_scores(j + 1, c)
        return carry

    lax.fori_loop(0, qi, body, 0)
    for c in range(2 * nh):
        consume(qi, c, True)
    lam = lam_ref[0]
    nw = nw_ref[...]
    for h in range(nh):
        a1, a2 = acc_ref[2 * h], acc_ref[2 * h + 1]
        o_t = a1[:LANES] / a1[LANES:LANES + 1] - lam * (a2[:LANES] / a2[LANES:LANES + 1])
        o_ref[:, h * LANES:(h + 1) * LANES] = (_head_norm(o_t.T, nw) * out_scale).astype(BF16)


def _differential(lam, q, k, vt, nw, out_scale):
    bsz, seq, width = q.shape
    nh = width // LANES
    nt = seq // TQ_DIFF
    q_spec = pl.BlockSpec((None, TQ_DIFF, width), lambda b, i: (b, i, 0))
    return pl.pallas_call(
        functools.partial(_diff_kernel, out_scale=out_scale),
        grid=(bsz, nt),
        in_specs=[pl.BlockSpec(memory_space=pltpu.SMEM), q_spec,
                  pl.BlockSpec((None, seq, width), lambda b, i: (b, 0, 0)),
                  pl.BlockSpec((None, nt, nh, LANES, TQ_DIFF), lambda b, i: (b, 0, 0, 0, 0)),
                  pl.BlockSpec((1, LANES), lambda b, i: (0, 0))],
        out_specs=q_spec,
        out_shape=jax.ShapeDtypeStruct((bsz, seq, width), BF16),
        scratch_shapes=[pltpu.VMEM((2 * nh, LANES + ONES_ROWS, TQ_DIFF), F32),
                        pltpu.VMEM((2 * nh, TQ_DIFF), F32),
                        pltpu.VMEM((2 * nh, TQ_DIFF, TQ_DIFF), F32)],
        compiler_params=pltpu.CompilerParams(
            dimension_semantics=("arbitrary", "arbitrary"), vmem_limit_bytes=VMEM_LIMIT),
        name="differential",
    )(lam, q, k, vt, nw)


def _out_ffn_kernel(x_ref, oa_ref, ob_ref, mod_ref, nw_ref, fw_ref, woa_ref, wob_ref,
                    wi_ref, cw_ref, cb_ref, wo_ref,
                    y_ref, act_ref, carry_ref, *, final):
    t = pl.program_id(1)
    m = mod_ref[...]
    d = D_MODEL
    g1, sh2, sc2, g2 = m[:, 2 * d:3 * d], m[:, 3 * d:4 * d], m[:, 4 * d:5 * d], m[:, 5 * d:6 * d]
    mix = _dot(oa_ref[...], woa_ref[...]) + _dot(ob_ref[...], wob_ref[...])
    x1 = x_ref[...] + g1 * mix
    h = _modulated_norm(x1, nw_ref[...], sc2, sh2).astype(BF16)

    @pl.when(t == 0)
    def _():
        carry_ref[...] = jnp.zeros_like(carry_ref)

    tm = x1.shape[0]
    row = lax.broadcasted_iota(jnp.int32, (tm, FF_CHUNK), 0)
    for c in range(D_FF // FF_CHUNK):
        cols = slice(c * FF_CHUNK, (c + 1) * FF_CHUNK)
        a = _dot(h, wi_ref[:, cols])
        u = _dot(h, wi_ref[:, D_FF + c * FF_CHUNK:D_FF + (c + 1) * FF_CHUNK])
        tail = carry_ref[:, cols]
        cw = cw_ref[:, cols]
        prev1 = jnp.where(row < 1, tail[7:8, :], pltpu.roll(a, 1, 0))
        prev2 = jnp.where(row < 1, tail[6:7, :], jnp.where(row < 2, tail[7:8, :], pltpu.roll(a, 2, 0)))
        carry_ref[:, cols] = a[tm - 8:tm, :]
        conv = prev2 * cw[0:1, :] + prev1 * cw[1:2, :] + a * cw[2:3, :] + cb_ref[:, cols]
        act_ref[:, cols] = (_silu(conv) * u).astype(BF16)
    y = x1 + g2 * _dot(act_ref[...], wo_ref[...])
    if final:
        ms = jnp.mean(y * y, axis=-1, keepdims=True)
        y = (y * lax.rsqrt(ms + NORM_EPS)) * fw_ref[...]
    y_ref[...] = y


def _out_ffn(x, oa, ob, mod_l, nw, fw, woa, wob, wi, cw, cb, wo, final):
    bsz, seq, _ = x.shape

    def resident(shape):
        nd = len(shape)
        return pl.BlockSpec(shape, lambda b, t: (0,) * nd, pipeline_mode=pl.Buffered(1))

    return pl.pallas_call(
        functools.partial(_out_ffn_kernel, final=final),
        grid=(bsz, seq // TM_PROJ),
        in_specs=[_tok_spec(D_MODEL), _tok_spec(MIX_HALF), _tok_spec(MIX_HALF),
                  pl.BlockSpec((None, 1, 6 * D_MODEL), lambda b, t: (b, 0, 0)),
                  _const_spec(nw.shape), _const_spec(fw.shape),
                  resident(woa.shape), resident(wob.shape), resident(wi.shape),
                  _const_spec(cw.shape), _const_spec(cb.shape), resident(wo.shape)],
        out_specs=_tok_spec(D_MODEL),
        out_shape=jax.ShapeDtypeStruct(x.shape, F32),
        scratch_shapes=[pltpu.VMEM((TM_PROJ, D_FF), BF16), pltpu.VMEM((8, D_FF), F32)],
        compiler_params=pltpu.CompilerParams(
            dimension_semantics=("arbitrary", "arbitrary"), vmem_limit_bytes=VMEM_LIMIT),
        name="out_ffn",
    )(x, oa, ob, mod_l, nw, fw, woa, wob, wi, cw, cb, wo)


def _rope_lane_table(positions):
    half = ROT_DIMS // 2
    inv_freq = ROPE_THETA ** (-jnp.arange(0, ROT_DIMS, 2, dtype=F32) / ROT_DIMS)
    ang = inv_freq[:, None] * positions.astype(F32).reshape(1, -1)
    cs = jnp.concatenate([jnp.cos(ang), jnp.sin(ang)], axis=0)
    d = np.arange(LANES) % HEAD_DIM
    sel = np.zeros((2 * half, 2 * LANES), np.float32)
    base = np.zeros((1, 2 * LANES), np.float32)
    for lane in range(LANES):
        if d[lane] < ROT_DIMS:
            sel[d[lane] % half, lane] = 1.0
            sel[half + d[lane] % half, LANES + lane] = -1.0 if d[lane] < half else 1.0
        else:
            base[0, lane] = 1.0
    tab = lax.dot_general(cs, jnp.asarray(sel), TN_DIMS, precision=lax.Precision.HIGHEST) + base
    return tab.reshape(positions.shape + (2 * LANES,))


def _cols(w, start, width):
    return w[:, start:start + width].astype(BF16)


SWA_HEAD_ORDER = (0, 4, 1, 5, 2, 6, 3, 7)


def kernel(x, c, positions, mod_w, mod_b, norm_mix_w, norm_ffn_w, ev_w_in, gla_gate_w, gla_gate_b,
           gla_norm_w, swa_sinks, ev_w_out, od_w_in, diff_lambda, diff_norm_w, hgrn_lb_logits,
           hgrn_norm_w, od_w_out, ffn_w_in, ffn_conv_w, ffn_conv_b, ffn_w_out, final_norm_w):
    bsz, seq, _ = x.shape
    table = _rope_lane_table(positions)
    lbs = jnp.cumsum(jax.nn.softmax(hgrn_lb_logits.astype(F32), axis=0), axis=0)
    lbs = lbs - lbs[0:1]
    mod = _modulation(c, mod_w, mod_b).reshape(DEPTH, bsz, 1, 6 * D_MODEL)
    swa_cols = jnp.concatenate([jnp.arange(HEAD_DIM) + HEAD_DIM * h for h in SWA_HEAD_ORDER])
    fw = final_norm_w.reshape(1, D_MODEL)

    for l in range(DEPTH):
        j = l // 2
        nw_mix = norm_mix_w[l].reshape(1, D_MODEL)
        if l % 2 == 0:
            w = ev_w_in[j]
            w_sq = w[:, 1552:2064][:, swa_cols].astype(BF16)
            w_glr = jnp.pad(w[:, 1536:1552], ((0, 0), (0, LANES - GLA_GATE_RANK))).astype(BF16)
            gate_w = jnp.pad(gla_gate_w[j], ((0, LANES - GLA_GATE_RANK), (0, 0))).astype(BF16)
            weights = [_cols(w, 0, 1536), w_glr, gate_w, gla_gate_b[j].reshape(1, -1), w_sq, _cols(w, 2064, 256)]
            outs = [(256, F32), (256, F32), (512, BF16), (512, F32), (256, F32), (512, BF16), (256, BF16)]
            gq, gk, gv, gr, ga, sq, skv = _in_projection(
                _even_in_kernel, "even_in", x, mod[l], nw_mix, table, [], weights, outs)
            oa = _gated_linear(gq, gk, ga, gv, gr, gla_norm_w[j].reshape(1, LANES), heads=2)
            ob = _sliding_window(swa_sinks[j], sq, skv)
            w_out = ev_w_out[j]
            wob = w_out[MIX_HALF:][swa_cols].astype(BF16)
        else:
            w = od_w_in[j]
            lb = lbs[j].reshape(1, -1)
            consts = [jnp.log(lb), jnp.log1p(-lb), 1.0 - lb]
            weights = [w.astype(BF16)]
            tiles = TM_PROJ // TQ_DIFF
            n_dh = MIX_HALF // LANES
            vt_out = ((bsz, seq // TQ_DIFF, n_dh, LANES, TQ_DIFF),
                      pl.BlockSpec((None, tiles, n_dh, LANES, TQ_DIFF), lambda b, t: (b, t, 0, 0, 0)), BF16)
            outs = [(512, BF16), (512, BF16), vt_out, (512, F32), (512, F32), (512, F32),
                    (512, BF16), (512, F32)]
            dq, dk, dvt, hq, hk, hf, hi, hg = _in_projection(
                _odd_in_kernel, "odd_in", x, mod[l], nw_mix, table, consts, weights, outs)
            lam_init = 0.8 - 0.6 * math.exp(-0.3 * l)
            lv = diff_lambda[j].astype(F32)
            lam = jnp.exp(jnp.sum(lv[0] * lv[1])) - jnp.exp(jnp.sum(lv[2] * lv[3])) + lam_init
            oa = _differential(lam.reshape(1), dq, dk, dvt, diff_norm_w[j].reshape(1, LANES), 1.0 - lam_init)
            ob = _gated_linear(hq, hk, hf, hi, hg, hgrn_norm_w[j].reshape(1, LANES), heads=1)
            w_out = od_w_out[j]
            wob = w_out[MIX_HALF:].astype(BF16)
        woa = w_out[:MIX_HALF].astype(BF16)
        x = _out_ffn(x, oa, ob, mod[l], norm_ffn_w[l].reshape(1, D_MODEL), fw, woa, wob,
                     ffn_w_in[l].astype(BF16), ffn_conv_w[l], ffn_conv_b[l].reshape(1, D_FF),
                     ffn_w_out[l].astype(BF16), final=(l == DEPTH - 1))
    return x
```

```python
import functools
import math

import jax
import numpy as np
import jax.numpy as jnp
from jax import lax
from jax.experimental import pallas as pl
from jax.experimental.pallas import tpu as pltpu

F32 = jnp.float32
BF16 = jnp.bfloat16

D_MODEL = 1024
DEPTH = 4
HEAD_DIM = 64
ROT_DIMS = HEAD_DIM // 4
ROPE_THETA = 500000.0
GLA_GATE_RANK = 16
GLA_GATE_NORMALIZER = 16.0
N_SWA_HEADS = 8
N_SWA_KV_HEADS = 2
WINDOW = 128
HGRN_EXPAND = 128
D_FF = 2816
MIX_HALF = D_MODEL // 2
NORM_EPS = 1e-6
LOG2_E = math.log2(math.e)

LANES = 128
CHUNK = 64
SUB = 32
EXP_CLAMP = 60.0
TM_PROJ = 512
TS_GL = 2048
TQ_DIFF = 256
SWA_BLOCKS = 4
ONES_ROWS = 16
FF_CHUNK = 256
VMEM_LIMIT = 56 * 1024 * 1024

NT_DIMS = (((1,), (1,)), ((), ()))
TN_DIMS = (((0,), (0,)), ((), ()))


def _sigmoid(z):
    return 1.0 / (1.0 + jnp.exp(-z))


def _silu(z):
    return z * _sigmoid(z)


def _log_sigmoid(z):
    return jnp.minimum(z, 0.0) - jnp.log(1.0 + jnp.exp(-jnp.abs(z)))


def _dot(a, b):
    return jnp.dot(a, b, preferred_element_type=F32)


def _modulated_norm(x, nw, sc, sh):
    ms = jnp.mean(x * x, axis=-1, keepdims=True)
    return ((x * lax.rsqrt(ms + NORM_EPS)) * nw) * (1.0 + sc) + sh


def _head_norm(o, nw):
    ms = jnp.mean(o * o, axis=-1, keepdims=True)
    return (o * lax.rsqrt(ms + NORM_EPS)) * nw


def _rope_tables(tab):
    c, s = tab[:, :LANES], tab[:, LANES:]
    lane = lax.broadcasted_iota(jnp.int32, c.shape, 1)
    first_half = (lane & (HEAD_DIM - 1)) < ROT_DIMS // 2
    return c, jnp.where(first_half, s, 0.0), jnp.where(first_half, 0.0, s)


def _rope(x, tables):
    n = x.shape[-1] // LANES
    c, s_up, s_down = (jnp.concatenate([t] * n, axis=-1) if n > 1 else t for t in tables)
    half = ROT_DIMS // 2
    up = pltpu.roll(x, x.shape[-1] - half, 1)
    down = pltpu.roll(x, half, 1)
    return x * c + up * s_up + down * s_down


def _mod_kernel(c_ref, w_ref, b_ref, o_ref):
    ca = _silu(c_ref[...]).astype(BF16)
    o_ref[0] = _dot(ca, w_ref[0].astype(BF16)) + b_ref[0]


def _modulation(c, mod_w, mod_b):
    bsz = c.shape[0]
    tn = 1024
    n_out = mod_w.shape[-1]
    return pl.pallas_call(
        _mod_kernel,
        grid=(DEPTH, n_out // tn),
        in_specs=[
            pl.BlockSpec((bsz, D_MODEL), lambda l, j: (0, 0)),
            pl.BlockSpec((1, D_MODEL, tn), lambda l, j: (l, 0, j)),
            pl.BlockSpec((1, 1, tn), lambda l, j: (l, 0, j)),
        ],
        out_specs=pl.BlockSpec((1, bsz, tn), lambda l, j: (l, 0, j)),
        out_shape=jax.ShapeDtypeStruct((DEPTH, bsz, n_out), F32),
        name="modulation",
    )(c, mod_w, mod_b.reshape(DEPTH, 1, n_out))


def _even_in_kernel(x_ref, mod_ref, nw_ref, tab_ref,
                    wg_ref, wl_ref, gw_ref, gb_ref, wsq_ref, wskv_ref,
                    gq_ref, gk_ref, gv_ref, gr_ref, ga_ref, sq_ref, skv_ref):
    m = mod_ref[...]
    h = _modulated_norm(x_ref[...], nw_ref[...], m[:, D_MODEL:2 * D_MODEL], m[:, 0:D_MODEL]).astype(BF16)
    glr = _dot(h, wl_ref[...]).astype(BF16)
    gq = _dot(h, wg_ref[:, 0:256])
    gk = _dot(h, wg_ref[:, 256:512])
    gv = _dot(h, wg_ref[:, 512:1024])
    gr = _dot(h, wg_ref[:, 1024:1536])
    sq = _dot(h, wsq_ref[...])
    skv = _dot(h, wskv_ref[...])
    gate = _dot(glr, gw_ref[...]) + gb_ref[...]
    gq_ref[...] = gq * (HEAD_DIM ** -0.5)
    gk_ref[...] = gk
    gv_ref[...] = gv.astype(BF16)
    gr_ref[...] = _silu(gr)
    ga_ref[...] = _log_sigmoid(gate) / GLA_GATE_NORMALIZER
    tab = _rope_tables(tab_ref[...])
    sq_ref[...] = (_rope(sq, tab) * (LOG2_E * HEAD_DIM ** -0.5)).astype(BF16)
    sk = _rope(skv[:, :LANES], tab)
    skv_ref[...] = jnp.concatenate([sk, skv[:, LANES:]], axis=-1).astype(BF16)


def _odd_in_kernel(x_ref, mod_ref, nw_ref, tab_ref, llb_ref, l1m_ref, oml_ref, w_ref,
                   dq_ref, dk_ref, dvt_ref, hq_ref, hk_ref, hf_ref, hi_ref, hg_ref):
    wdq_ref, wdk_ref, wdv_ref, whq_ref, whf_ref, whi_ref, whg_ref = (
        w_ref.at[:, i * MIX_HALF:(i + 1) * MIX_HALF] for i in range(7))
    m = mod_ref[...]
    h = _modulated_norm(x_ref[...], nw_ref[...], m[:, D_MODEL:2 * D_MODEL], m[:, 0:D_MODEL]).astype(BF16)
    tab = _rope_tables(tab_ref[...])
    z, hq, hg, dq, dk, dv, hi = (_dot(h, w[...]) for w in
                                 (whf_ref, whq_ref, whg_ref, wdq_ref, wdk_ref, wdv_ref, whi_ref))
    a1 = llb_ref[...]
    a2 = l1m_ref[...] + _log_sigmoid(z)
    hf_ref[...] = jnp.maximum(a1, a2) + jnp.log(1.0 + jnp.exp(-jnp.abs(a1 - a2)))
    hk_ref[...] = oml_ref[...] * _sigmoid(-z)
    hq_ref[...] = _silu(hq) * (HGRN_EXPAND ** -0.5)
    hg_ref[...] = _silu(hg)
    dq_ref[...] = (_rope(dq, tab) * (LOG2_E * HEAD_DIM ** -0.5)).astype(BF16)
    dk_ref[...] = _rope(dk, tab).astype(BF16)
    for t in range(dvt_ref.shape[0]):
        for hd in range(dvt_ref.shape[1]):
            tile = dv[t * TQ_DIFF:(t + 1) * TQ_DIFF, hd * LANES:(hd + 1) * LANES]
            dvt_ref[t, hd] = tile.T.astype(BF16)
    hi_ref[...] = hi.astype(BF16)


def _tok_spec(width):
    return pl.BlockSpec((None, TM_PROJ, width), lambda b, t: (b, t, 0))


def _const_spec(shape):
    nd = len(shape)
    return pl.BlockSpec(shape, lambda b, t: (0,) * nd)


def _in_projection(kern, name, x, mod_l, nw, table, consts, weights, outs):
    bsz, seq, _ = x.shape
    in_specs = ([_tok_spec(D_MODEL), pl.BlockSpec((None, 1, 6 * D_MODEL), lambda b, t: (b, 0, 0)),
                 _const_spec(nw.shape), _tok_spec(2 * LANES)]
                + [_const_spec(a.shape) for a in consts]
                + [_const_spec(w.shape) for w in weights])
    out_specs = [_tok_spec(o[0]) if len(o) == 2 else o[1] for o in outs]
    out_shape = [jax.ShapeDtypeStruct((bsz, seq, o[0]) if len(o) == 2 else o[0], o[-1]) for o in outs]
    return pl.pallas_call(
        kern,
        grid=(bsz, seq // TM_PROJ),
        in_specs=in_specs,
        out_specs=out_specs,
        out_shape=out_shape,
        compiler_params=pltpu.CompilerParams(
            dimension_semantics=("arbitrary", "arbitrary"), vmem_limit_bytes=VMEM_LIMIT),
        name=name,
    )(x, mod_l, nw, table, *consts, *weights)


def _gl_kernel(q_ref, k_ref, g_ref, v_ref, gate_ref, nw_ref, o_ref, st_ref, cum_ref, inter_ref, *, heads):
    @pl.when(pl.program_id(2) == 0)
    def _():
        st_ref[...] = jnp.zeros_like(st_ref)

    dk = LANES // heads
    row = lax.broadcasted_iota(jnp.int32, (CHUNK, CHUNK), 0)
    col = lax.broadcasted_iota(jnp.int32, (CHUNK, CHUNK), 1)
    causal = row >= col
    tri = jnp.where(causal, 1.0, 0.0).astype(BF16)
    lane_sub = lax.broadcasted_iota(jnp.int32, (SUB, LANES), 1)
    lane_chunk = lax.broadcasted_iota(jnp.int32, (CHUNK, LANES), 1)
    nw = nw_ref[...]
    n_sub = CHUNK // SUB
    zeros_sub = jnp.zeros((SUB, LANES), BF16)

    def own(t, lane, p):
        return t if heads == 1 else jnp.where((lane >= p * dk) & (lane < (p + 1) * dk), t, 0.0)

    chunks = [slice(c * CHUNK, (c + 1) * CHUNK) for c in range(q_ref.shape[0] // CHUNK)]
    heads_r = range(heads)

    cum = []
    spread = jnp.zeros((1, LANES), F32)
    for rows in chunks:
        g = g_ref[rows, :]
        for a in range(n_sub):
            spread = jnp.maximum(spread, -jnp.sum(g[SUB * a:SUB * (a + 1)], axis=0, keepdims=True))
        g1 = g.astype(BF16)
        r1 = g - g1.astype(F32)
        g2 = r1.astype(BF16)
        g3 = (r1 - g2.astype(F32)).astype(BF16)
        cum.append(_dot(tri, g1) + _dot(tri, g2) + _dot(tri, g3))

    q_in, decay_tot, k_own, k_stack, q_stack = [], [], [], [], []
    for rows, b in zip(chunks, cum):
        q = q_ref[rows, :]
        k = k_ref[rows, :]
        btot = b[CHUNK - 1:CHUNK, :]
        q_in.append((q * jnp.exp(b)).astype(BF16))
        k_out = k * jnp.exp(btot - b)
        k_own.append([own(k_out, lane_chunk, p).astype(BF16) for p in heads_r])
        decay_tot.append(jnp.exp(btot))
        k_groups, q_groups = [], []
        for a in range(n_sub):
            hi = SUB * (a + 1)
            lvl = b[SUB * a - 1:SUB * a, :] if a else jnp.zeros((1, LANES), F32)
            kk = (k[:hi] * jnp.exp(jnp.minimum(lvl - b[:hi], EXP_CLAMP))).astype(BF16)
            if hi < CHUNK:
                kk = jnp.concatenate([kk, jnp.zeros((CHUNK - hi, LANES), BF16)], axis=0)
            k_groups.append(kk)
            q_groups.append(q[SUB * a:hi] * jnp.exp(b[SUB * a:hi] - lvl))
        k_stack.append(jnp.concatenate(k_groups, axis=1))
        per_head = []
        for p in heads_r:
            q_rows = []
            for a in range(n_sub):
                qa = own(q_groups[a], lane_sub, p).astype(BF16)
                q_rows.append(jnp.concatenate([zeros_sub] * a + [qa] + [zeros_sub] * (n_sub - 1 - a), axis=1))
            per_head.append(jnp.concatenate(q_rows, axis=0))
        q_stack.append(per_head)

    vals = [[v_ref[rows, p * LANES:(p + 1) * LANES] for p in heads_r] for rows in chunks]
    scores = [[lax.dot_general(q_stack[c][p], k_stack[c], NT_DIMS, preferred_element_type=F32)
               for p in heads_r] for c in range(len(chunks))]
    incr = [[lax.dot_general(vals[c][p], k_own[c][p], TN_DIMS, preferred_element_type=F32)
             for p in heads_r] for c in range(len(chunks))]

    outs = []
    for p in heads_r:
        st = st_ref[p]
        for c in range(len(chunks)):
            s = jnp.where(causal, scores[c][p], 0.0).astype(BF16)
            inter = lax.dot_general(q_in[c], st.astype(BF16), NT_DIMS, preferred_element_type=F32)
            inter_ref[p, chunks[c], :] = inter
            o = _dot(s, vals[c][p]) + inter
            outs.append((c, p, o))
            st = st * decay_tot[c] + incr[c][p]
        st_ref[p] = st

    def finish(c, p, o):
        cols = slice(p * LANES, (p + 1) * LANES)
        o_ref[chunks[c], cols] = (_head_norm(o, nw) * gate_ref[chunks[c], cols]).astype(BF16)

    for c, p, o in outs:
        finish(c, p, o)

    for c, b in enumerate(cum):
        cum_ref[chunks[c], :] = b

    @pl.when(jnp.max(spread) > EXP_CLAMP)
    def _():
        for c, rows in enumerate(chunks):
            q = q_ref[rows, :]
            b = cum_ref[rows, :]

            def key_column(j, acc):
                kj = k_ref[pl.ds(c * CHUNK + j, 1), :]
                bj = cum_ref[pl.ds(c * CHUNK + j, 1), :]
                t = q * kj * jnp.exp(jnp.minimum(b - bj, 0.0))
                return tuple(jnp.where(col == j, jnp.sum(own(t, lane_chunk, p), axis=1, keepdims=True), acc[p])
                             for p in heads_r)

            exact = lax.fori_loop(0, CHUNK, key_column, tuple(jnp.zeros((CHUNK, CHUNK), F32) for _ in heads_r))
            for p in heads_r:
                s = jnp.where(causal, exact[p], 0.0).astype(BF16)
                finish(c, p, _dot(s, v_ref[rows, p * LANES:(p + 1) * LANES]) + inter_ref[p, rows, :])


def _gated_linear(q, k, g, v, gate, nw, heads):
    bsz, seq, width = q.shape
    groups = width // LANES
    vw = heads * LANES
    qspec = pl.BlockSpec((None, TS_GL, LANES), lambda b, gi, s: (b, s, gi))
    vspec = pl.BlockSpec((None, TS_GL, vw), lambda b, gi, s: (b, s, gi))
    return pl.pallas_call(
        functools.partial(_gl_kernel, heads=heads),
        grid=(bsz, groups, seq // TS_GL),
        in_specs=[qspec, qspec, qspec, vspec, vspec, pl.BlockSpec((1, LANES), lambda b, gi, s: (0, 0))],
        out_specs=vspec,
        out_shape=jax.ShapeDtypeStruct((bsz, seq, groups * vw), BF16),
        scratch_shapes=[pltpu.VMEM((heads, LANES, LANES), F32), pltpu.VMEM((TS_GL, LANES), F32),
                        pltpu.VMEM((heads, TS_GL, LANES), F32)],
        compiler_params=pltpu.CompilerParams(
            dimension_semantics=("arbitrary", "arbitrary", "arbitrary"), vmem_limit_bytes=VMEM_LIMIT),
        name=f"gated_linear_{heads}",
    )(q, k, g, v, gate, nw)


def _swa_kernel(sink_ref, q_ref, cur_ref, prev_ref, o_ref):
    n = pl.program_id(1)
    kj = lax.broadcasted_iota(jnp.int32, (2 * WINDOW, WINDOW), 0)
    qi = lax.broadcasted_iota(jnp.int32, (2 * WINDOW, WINDOW), 1) + WINDOW
    rel = qi - kj
    band = (rel >= 0) & (rel < WINDOW)
    band_first = band & ((n > 0) | (kj >= WINDOW))
    lane = lax.broadcasted_iota(jnp.int32, (WINDOW, LANES), 1)
    low = lane < HEAD_DIM
    per_kv = N_SWA_HEADS // N_SWA_KV_HEADS
    chains = [(t, i, kv) for t in range(q_ref.shape[0] // WINDOW)
              for i in range(per_kv) for kv in range(N_SWA_KV_HEADS)]
    values, scores = {}, []
    for t, i, kv in chains:
        rows = slice(t * WINDOW, (t + 1) * WINDOW)
        if (i, kv) == (0, 0):
            cur = cur_ref[rows, :]
            prev = cur_ref[(t - 1) * WINDOW:t * WINDOW, :] if t else prev_ref[...]
            keys = jnp.concatenate([prev[:, :LANES], cur[:, :LANES]], axis=0)
            vals = jnp.concatenate([prev[:, LANES:], cur[:, LANES:]], axis=0)
            vlow = lax.broadcasted_iota(jnp.int32, vals.shape, 1) < HEAD_DIM
            one = jnp.ones_like(vals)
            values[t] = (jnp.where(vlow, vals, one), jnp.where(vlow, one, vals))
        qg = q_ref[rows, i * LANES:(i + 1) * LANES]
        qm = jnp.where(low if kv == 0 else ~low, qg, jnp.zeros_like(qg))
        scores.append(lax.dot_general(keys, qm, NT_DIMS, preferred_element_type=F32))
    probs, sink_terms = [], []
    for (t, i, kv), s in zip(chains, scores):
        sink = sink_ref[kv * per_kv + i] * LOG2_E
        s = jnp.where(band if t else band_first, s, -jnp.inf)
        m = jnp.maximum(jnp.max(s, axis=0, keepdims=True), sink)
        probs.append(jnp.exp2(s - m).astype(BF16))
        sink_terms.append(jnp.exp2(sink - m))
    outs = []
    for (t, _, kv), p, st in zip(chains, probs, sink_terms):
        pv = lax.dot_general(values[t][kv], p, TN_DIMS, preferred_element_type=F32)
        denom_row = HEAD_DIM if kv == 0 else 0
        outs.append(pv * (1.0 / (pv[denom_row:denom_row + 1, :] + st)))
    top = lax.broadcasted_iota(jnp.int32, (LANES, WINDOW), 0) < HEAD_DIM
    for idx in range(0, len(chains), N_SWA_KV_HEADS):
        t, i, _ = chains[idx]
        o_t = jnp.where(top, outs[idx], outs[idx + 1])
        o_ref[t * WINDOW:(t + 1) * WINDOW, i * LANES:(i + 1) * LANES] = o_t.T.astype(BF16)


def _sliding_window(sinks, q, kv):
    bsz, seq, _ = q.shape
    tq = SWA_BLOCKS * WINDOW
    return pl.pallas_call(
        _swa_kernel,
        grid=(bsz, seq // tq),
        in_specs=[
            pl.BlockSpec(memory_space=pltpu.SMEM),
            pl.BlockSpec((None, tq, MIX_HALF), lambda b, n: (b, n, 0)),
            pl.BlockSpec((None, tq, 2 * LANES), lambda b, n: (b, n, 0)),
            pl.BlockSpec((None, WINDOW, 2 * LANES), lambda b, n: (b, jnp.maximum(SWA_BLOCKS * n - 1, 0), 0)),
        ],
        out_specs=pl.BlockSpec((None, tq, MIX_HALF), lambda b, n: (b, n, 0)),
        out_shape=jax.ShapeDtypeStruct((bsz, seq, MIX_HALF), BF16),
        compiler_params=pltpu.CompilerParams(
            dimension_semantics=("arbitrary", "arbitrary"), vmem_limit_bytes=VMEM_LIMIT),
        name="sliding_window",
    )(sinks, q, kv, kv)


def _diff_kernel(lam_ref, q_ref, k_ref, vt_ref, nw_ref, o_ref, acc_ref, max_ref, s_ref, *, out_scale):
    qi = pl.program_id(1)
    tq = q_ref.shape[0]
    nh = q_ref.shape[1] // LANES
    lane = lax.broadcasted_iota(jnp.int32, (tq, LANES), 1)
    qs = []
    for h in range(nh):
        q = q_ref[:, h * LANES:(h + 1) * LANES]
        zero = jnp.zeros_like(q)
        qs += [jnp.where(lane < HEAD_DIM, q, zero), jnp.where(lane >= HEAD_DIM, q, zero)]
    key = lax.broadcasted_iota(jnp.int32, (tq, tq), 0)
    qry = lax.broadcasted_iota(jnp.int32, (tq, tq), 1)
    causal = key <= qry
    ones = jnp.ones((acc_ref.shape[1] - LANES, tq), BF16)
    acc_ref[...] = jnp.zeros_like(acc_ref)
    max_ref[...] = jnp.full(max_ref.shape, -jnp.inf, F32)

    def issue_scores(j, c):
        rows = pl.ds(pl.multiple_of(j * tq, tq), tq)
        kt = k_ref[rows, (c // 2) * LANES:(c // 2 + 1) * LANES]
        s_ref[c] = lax.dot_general(kt, qs[c], NT_DIMS, preferred_element_type=F32)

    def consume(j, c, masked):
        s = s_ref[c]
        if masked:
            s = jnp.where(causal, s, -jnp.inf)
        m = max_ref[c:c + 1, :]
        m_new = jnp.maximum(m, jnp.max(s, axis=0, keepdims=True))
        max_ref[c:c + 1, :] = m_new
        p = jnp.exp2(s - m_new).astype(BF16)
        vt1 = jnp.concatenate([vt_ref[j, c // 2], ones], axis=0)
        acc_ref[c] = jnp.exp2(m - m_new) * acc_ref[c] + _dot(vt1, p)

    for c in range(2 * nh):
        issue_scores(0, c)

    def body(j, carry):
        for c in range(2 * nh):
            consume(j, c, False)
            issue_scores(j + 1, c)
        return carry

    lax.fori_loop(0, qi, body, 0)
    for c in range(2 * nh):
        consume(qi, c, True)
    lam = lam_ref[0]
    nw = nw_ref[...]
    for h in range(nh):
        a1, a2 = acc_ref[2 * h], acc_ref[2 * h + 1]
        o_t = a1[:LANES] / a1[LANES:LANES + 1] - lam * (a2[:LANES] / a2[LANES:LANES + 1])
        o_ref[:, h * LANES:(h + 1) * LANES] = (_head_norm(o_t.T, nw) * out_scale).astype(BF16)


def _differential(lam, q, k, vt, nw, out_scale):
    bsz, seq, width = q.shape
    nh = width // LANES
    nt = seq // TQ_DIFF
    q_spec = pl.BlockSpec((None, TQ_DIFF, width), lambda b, i: (b, i, 0))
    return pl.pallas_call(
        functools.partial(_diff_kernel, out_scale=out_scale),
        grid=(bsz, nt),
        in_specs=[pl.BlockSpec(memory_space=pltpu.SMEM), q_spec,
                  pl.BlockSpec((None, seq, width), lambda b, i: (b, 0, 0)),
                  pl.BlockSpec((None, nt, nh, LANES, TQ_DIFF), lambda b, i: (b, 0, 0, 0, 0)),
                  pl.BlockSpec((1, LANES), lambda b, i: (0, 0))],
        out_specs=q_spec,
        out_shape=jax.ShapeDtypeStruct((bsz, seq, width), BF16),
        scratch_shapes=[pltpu.VMEM((2 * nh, LANES + ONES_ROWS, TQ_DIFF), F32),
                        pltpu.VMEM((2 * nh, TQ_DIFF), F32),
                        pltpu.VMEM((2 * nh, TQ_DIFF, TQ_DIFF), F32)],
        compiler_params=pltpu.CompilerParams(
            dimension_semantics=("arbitrary", "arbitrary"), vmem_limit_bytes=VMEM_LIMIT),
        name="differential",
    )(lam, q, k, vt, nw)


def _out_ffn_kernel(x_ref, oa_ref, ob_ref, mod_ref, nw_ref, fw_ref, woa_ref, wob_ref,
                    wi_ref, cw_ref, cb_ref, wo_ref,
                    y_ref, act_ref, carry_ref, *, final):
    t = pl.program_id(1)
    m = mod_ref[...]
    d = D_MODEL
    g1, sh2, sc2, g2 = m[:, 2 * d:3 * d], m[:, 3 * d:4 * d], m[:, 4 * d:5 * d], m[:, 5 * d:6 * d]
    mix = _dot(oa_ref[...], woa_ref[...]) + _dot(ob_ref[...], wob_ref[...])
    x1 = x_ref[...] + g1 * mix
    h = _modulated_norm(x1, nw_ref[...], sc2, sh2).astype(BF16)

    @pl.when(t == 0)
    def _():
        carry_ref[...] = jnp.zeros_like(carry_ref)

    tm = x1.shape[0]
    row = lax.broadcasted_iota(jnp.int32, (tm, FF_CHUNK), 0)
    for c in range(D_FF // FF_CHUNK):
        cols = slice(c * FF_CHUNK, (c + 1) * FF_CHUNK)
        a = _dot(h, wi_ref[:, cols])
        u = _dot(h, wi_ref[:, D_FF + c * FF_CHUNK:D_FF + (c + 1) * FF_CHUNK])
        tail = carry_ref[:, cols]
        cw = cw_ref[:, cols]
        prev1 = jnp.where(row < 1, tail[7:8, :], pltpu.roll(a, 1, 0))
        prev2 = jnp.where(row < 1, tail[6:7, :], jnp.where(row < 2, tail[7:8, :], pltpu.roll(a, 2, 0)))
        carry_ref[:, cols] = a[tm - 8:tm, :]
        conv = prev2 * cw[0:1, :] + prev1 * cw[1:2, :] + a * cw[2:3, :] + cb_ref[:, cols]
        act_ref[:, cols] = (_silu(conv) * u).astype(BF16)
    y = x1 + g2 * _dot(act_ref[...], wo_ref[...])
    if final:
        ms = jnp.mean(y * y, axis=-1, keepdims=True)
        y = (y * lax.rsqrt(ms + NORM_EPS)) * fw_ref[...]
    y_ref[...] = y


def _out_ffn(x, oa, ob, mod_l, nw, fw, woa, wob, wi, cw, cb, wo, final):
    bsz, seq, _ = x.shape

    def resident(shape):
        nd = len(shape)
        return pl.BlockSpec(shape, lambda b, t: (0,) * nd, pipeline_mode=pl.Buffered(1))

    return pl.pallas_call(
        functools.partial(_out_ffn_kernel, final=final),
        grid=(bsz, seq // TM_PROJ),
        in_specs=[_tok_spec(D_MODEL), _tok_spec(MIX_HALF), _tok_spec(MIX_HALF),
                  pl.BlockSpec((None, 1, 6 * D_MODEL), lambda b, t: (b, 0, 0)),
                  _const_spec(nw.shape), _const_spec(fw.shape),
                  resident(woa.shape), resident(wob.shape), resident(wi.shape),
                  _const_spec(cw.shape), _const_spec(cb.shape), resident(wo.shape)],
        out_specs=_tok_spec(D_MODEL),
        out_shape=jax.ShapeDtypeStruct(x.shape, F32),
        scratch_shapes=[pltpu.VMEM((TM_PROJ, D_FF), BF16), pltpu.VMEM((8, D_FF), F32)],
        compiler_params=pltpu.CompilerParams(
            dimension_semantics=("arbitrary", "arbitrary"), vmem_limit_bytes=VMEM_LIMIT),
        name="out_ffn",
    )(x, oa, ob, mod_l, nw, fw, woa, wob, wi, cw, cb, wo)


def _rope_lane_table(positions):
    half = ROT_DIMS // 2
    inv_freq = ROPE_THETA ** (-jnp.arange(0, ROT_DIMS, 2, dtype=F32) / ROT_DIMS)
    ang = inv_freq[:, None] * positions.astype(F32).reshape(1, -1)
    cs = jnp.concatenate([jnp.cos(ang), jnp.sin(ang)], axis=0)
    d = np.arange(LANES) % HEAD_DIM
    sel = np.zeros((2 * half, 2 * LANES), np.float32)
    base = np.zeros((1, 2 * LANES), np.float32)
    for lane in range(LANES):
        if d[lane] < ROT_DIMS:
            sel[d[lane] % half, lane] = 1.0
            sel[half + d[lane] % half, LANES + lane] = -1.0 if d[lane] < half else 1.0
        else:
            base[0, lane] = 1.0
    tab = lax.dot_general(cs, jnp.asarray(sel), TN_DIMS, precision=lax.Precision.HIGHEST) + base
    return tab.reshape(positions.shape + (2 * LANES,))


def _cols(w, start, width):
    return w[:, start:start + width].astype(BF16)


SWA_HEAD_ORDER = (0, 4, 1, 5, 2, 6, 3, 7)


def kernel(x, c, positions, mod_w, mod_b, norm_mix_w, norm_ffn_w, ev_w_in, gla_gate_w, gla_gate_b,
           gla_norm_w, swa_sinks, ev_w_out, od_w_in, diff_lambda, diff_norm_w, hgrn_lb_logits,
           hgrn_norm_w, od_w_out, ffn_w_in, ffn_conv_w, ffn_conv_b, ffn_w_out, final_norm_w):
    bsz, seq, _ = x.shape
    table = _rope_lane_table(positions)
    lbs = jnp.cumsum(jax.nn.softmax(hgrn_lb_logits.astype(F32), axis=0), axis=0)
    lbs = lbs - lbs[0:1]
    mod = _modulation(c, mod_w, mod_b).reshape(DEPTH, bsz, 1, 6 * D_MODEL)
    swa_cols = jnp.concatenate([jnp.arange(HEAD_DIM) + HEAD_DIM * h for h in SWA_HEAD_ORDER])
    fw = final_norm_w.reshape(1, D_MODEL)

    for l in range(DEPTH):
        j = l // 2
        nw_mix = norm_mix_w[l].reshape(1, D_MODEL)
        if l % 2 == 0:
            w = ev_w_in[j]
            w_sq = w[:, 1552:2064][:, swa_cols].astype(BF16)
            w_glr = jnp.pad(w[:, 1536:1552], ((0, 0), (0, LANES - GLA_GATE_RANK))).astype(BF16)
            gate_w = jnp.pad(gla_gate_w[j], ((0, LANES - GLA_GATE_RANK), (0, 0))).astype(BF16)
            weights = [_cols(w, 0, 1536), w_glr, gate_w, gla_gate_b[j].reshape(1, -1), w_sq, _cols(w, 2064, 256)]
            outs = [(256, F32), (256, F32), (512, BF16), (512, F32), (256, F32), (512, BF16), (256, BF16)]
            gq, gk, gv, gr, ga, sq, skv = _in_projection(
                _even_in_kernel, "even_in", x, mod[l], nw_mix, table, [], weights, outs)
            oa = _gated_linear(gq, gk, ga, gv, gr, gla_norm_w[j].reshape(1, LANES), heads=2)
            ob = _sliding_window(swa_sinks[j], sq, skv)
            w_out = ev_w_out[j]
            wob = w_out[MIX_HALF:][swa_cols].astype(BF16)
        else:
            w = od_w_in[j]
            lb = lbs[j].reshape(1, -1)
            consts = [jnp.log(lb), jnp.log1p(-lb), 1.0 - lb]
            weights = [w.astype(BF16)]
            tiles = TM_PROJ // TQ_DIFF
            n_dh = MIX_HALF // LANES
            vt_out = ((bsz, seq // TQ_DIFF, n_dh, LANES, TQ_DIFF),
                      pl.BlockSpec((None, tiles, n_dh, LANES, TQ_DIFF), lambda b, t: (b, t, 0, 0, 0)), BF16)
            outs = [(512, BF16), (512, BF16), vt_out, (512, F32), (512, F32), (512, F32),
                    (512, BF16), (512, F32)]
            dq, dk, dvt, hq, hk, hf, hi, hg = _in_projection(
                _odd_in_kernel, "odd_in", x, mod[l], nw_mix, table, consts, weights, outs)
            lam_init = 0.8 - 0.6 * math.exp(-0.3 * l)
            lv = diff_lambda[j].astype(F32)
            lam = jnp.exp(jnp.sum(lv[0] * lv[1])) - jnp.exp(jnp.sum(lv[2] * lv[3])) + lam_init
            oa = _differential(lam.reshape(1), dq, dk, dvt, diff_norm_w[j].reshape(1, LANES), 1.0 - lam_init)
            ob = _gated_linear(hq, hk, hf, hi, hg, hgrn_norm_w[j].reshape(1, LANES), heads=1)
            w_out = od_w_out[j]
            wob = w_out[MIX_HALF:].astype(BF16)
        woa = w_out[:MIX_HALF].astype(BF16)
        x = _out_ffn(x, oa, ob, mod[l], norm_ffn_w[l].reshape(1, D_MODEL), fw, woa, wob,
                     ffn_w_in[l].astype(BF16), ffn_conv_w[l], ffn_conv_b[l].reshape(1, D_FF),
                     ffn_w_out[l].astype(BF16), final=(l == DEPTH - 1))
    return x
```

```python
import functools
import math

import jax
import numpy as np
import jax.numpy as jnp
from jax import lax
from jax.experimental import pallas as pl
from jax.experimental.pallas import tpu as pltpu

F32 = jnp.float32
BF16 = jnp.bfloat16

D_MODEL = 1024
DEPTH = 4
HEAD_DIM = 64
ROT_DIMS = HEAD_DIM // 4
ROPE_THETA = 500000.0
GLA_GATE_RANK = 16
GLA_GATE_NORMALIZER = 16.0
N_SWA_HEADS = 8
N_SWA_KV_HEADS = 2
WINDOW = 128
HGRN_EXPAND = 128
D_FF = 2816
MIX_HALF = D_MODEL // 2
NORM_EPS = 1e-6
LOG2_E = math.log2(math.e)

LANES = 128
CHUNK = 64
SUB = 32
EXP_CLAMP = 60.0
TM_PROJ = 512
TS_GL = 2048
TQ_DIFF = 512
SWA_BLOCKS = 4
ONES_ROWS = 16
FF_CHUNK = 256
VMEM_LIMIT = 56 * 1024 * 1024

NT_DIMS = (((1,), (1,)), ((), ()))
TN_DIMS = (((0,), (0,)), ((), ()))


def _sigmoid(z):
    return 1.0 / (1.0 + jnp.exp(-z))


def _silu(z):
    return z * _sigmoid(z)


def _log_sigmoid(z):
    return jnp.minimum(z, 0.0) - jnp.log(1.0 + jnp.exp(-jnp.abs(z)))


def _dot(a, b):
    return jnp.dot(a, b, preferred_element_type=F32)


def _modulated_norm(x, nw, sc, sh):
    ms = jnp.mean(x * x, axis=-1, keepdims=True)
    return ((x * lax.rsqrt(ms + NORM_EPS)) * nw) * (1.0 + sc) + sh


def _head_norm(o, nw):
    ms = jnp.mean(o * o, axis=-1, keepdims=True)
    return (o * lax.rsqrt(ms + NORM_EPS)) * nw


def _rope_tables(tab):
    c, s = tab[:, :LANES], tab[:, LANES:]
    lane = lax.broadcasted_iota(jnp.int32, c.shape, 1)
    first_half = (lane & (HEAD_DIM - 1)) < ROT_DIMS // 2
    return c, jnp.where(first_half, s, 0.0), jnp.where(first_half, 0.0, s)


def _rope(x, tables):
    n = x.shape[-1] // LANES
    c, s_up, s_down = (jnp.concatenate([t] * n, axis=-1) if n > 1 else t for t in tables)
    half = ROT_DIMS // 2
    up = pltpu.roll(x, x.shape[-1] - half, 1)
    down = pltpu.roll(x, half, 1)
    return x * c + up * s_up + down * s_down


def _mod_kernel(c_ref, w_ref, b_ref, o_ref):
    ca = _silu(c_ref[...]).astype(BF16)
    o_ref[0] = _dot(ca, w_ref[0].astype(BF16)) + b_ref[0]


def _modulation(c, mod_w, mod_b):
    bsz = c.shape[0]
    tn = 1024
    n_out = mod_w.shape[-1]
    return pl.pallas_call(
        _mod_kernel,
        grid=(DEPTH, n_out // tn),
        in_specs=[
            pl.BlockSpec((bsz, D_MODEL), lambda l, j: (0, 0)),
            pl.BlockSpec((1, D_MODEL, tn), lambda l, j: (l, 0, j)),
            pl.BlockSpec((1, 1, tn), lambda l, j: (l, 0, j)),
        ],
        out_specs=pl.BlockSpec((1, bsz, tn), lambda l, j: (l, 0, j)),
        out_shape=jax.ShapeDtypeStruct((DEPTH, bsz, n_out), F32),
        name="modulation",
    )(c, mod_w, mod_b.reshape(DEPTH, 1, n_out))


def _even_in_kernel(x_ref, mod_ref, nw_ref, tab_ref,
                    wg_ref, wl_ref, gw_ref, gb_ref, wsq_ref, wskv_ref,
                    gq_ref, gk_ref, gv_ref, gr_ref, ga_ref, sq_ref, skv_ref):
    m = mod_ref[...]
    h = _modulated_norm(x_ref[...], nw_ref[...], m[:, D_MODEL:2 * D_MODEL], m[:, 0:D_MODEL]).astype(BF16)
    glr = _dot(h, wl_ref[...]).astype(BF16)
    gq = _dot(h, wg_ref[:, 0:256])
    gk = _dot(h, wg_ref[:, 256:512])
    gv = _dot(h, wg_ref[:, 512:1024])
    gr = _dot(h, wg_ref[:, 1024:1536])
    sq = _dot(h, wsq_ref[...])
    skv = _dot(h, wskv_ref[...])
    gate = _dot(glr, gw_ref[...]) + gb_ref[...]
    gq_ref[...] = gq * (HEAD_DIM ** -0.5)
    gk_ref[...] = gk
    gv_ref[...] = gv.astype(BF16)
    gr_ref[...] = _silu(gr)
    ga_ref[...] = _log_sigmoid(gate) / GLA_GATE_NORMALIZER
    tab = _rope_tables(tab_ref[...])
    sq_ref[...] = (_rope(sq, tab) * (LOG2_E * HEAD_DIM ** -0.5)).astype(BF16)
    sk = _rope(skv[:, :LANES], tab)
    skv_ref[...] = jnp.concatenate([sk, skv[:, LANES:]], axis=-1).astype(BF16)


def _odd_in_kernel(x_ref, mod_ref, nw_ref, tab_ref, llb_ref, l1m_ref, oml_ref, w_ref,
                   dq_ref, dk_ref, dvt_ref, hq_ref, hk_ref, hf_ref, hi_ref, hg_ref):
    wdq_ref, wdk_ref, wdv_ref, whq_ref, whf_ref, whi_ref, whg_ref = (
        w_ref.at[:, i * MIX_HALF:(i + 1) * MIX_HALF] for i in range(7))
    m = mod_ref[...]
    h = _modulated_norm(x_ref[...], nw_ref[...], m[:, D_MODEL:2 * D_MODEL], m[:, 0:D_MODEL]).astype(BF16)
    tab = _rope_tables(tab_ref[...])
    z, hq, hg, dq, dk, dv, hi = (_dot(h, w[...]) for w in
                                 (whf_ref, whq_ref, whg_ref, wdq_ref, wdk_ref, wdv_ref, whi_ref))
    ez = jnp.exp(-jnp.abs(z))
    a1 = llb_ref[...]
    a2 = l1m_ref[...] + (jnp.minimum(z, 0.0) - jnp.log(1.0 + ez))
    hf_ref[...] = jnp.maximum(a1, a2) + jnp.log(1.0 + jnp.exp(-jnp.abs(a1 - a2)))
    hk_ref[...] = oml_ref[...] * (jnp.where(z >= 0.0, ez, 1.0) / (1.0 + ez))
    hq_ref[...] = _silu(hq) * (HGRN_EXPAND ** -0.5)
    hg_ref[...] = _silu(hg)
    dq_ref[...] = (_rope(dq, tab) * (LOG2_E * HEAD_DIM ** -0.5)).astype(BF16)
    dk_ref[...] = _rope(dk, tab).astype(BF16)
    for t in range(dvt_ref.shape[0]):
        for hd in range(dvt_ref.shape[1]):
            tile = dv[t * TQ_DIFF:(t + 1) * TQ_DIFF, hd * LANES:(hd + 1) * LANES]
            dvt_ref[t, hd] = tile.T.astype(BF16)
    hi_ref[...] = hi.astype(BF16)


def _tok_spec(width):
    return pl.BlockSpec((None, TM_PROJ, width), lambda b, t: (b, t, 0))


def _const_spec(shape):
    nd = len(shape)
    return pl.BlockSpec(shape, lambda b, t: (0,) * nd)


def _in_projection(kern, name, x, mod_l, nw, table, consts, weights, outs):
    bsz, seq, _ = x.shape
    in_specs = ([_tok_spec(D_MODEL), pl.BlockSpec((None, 1, 6 * D_MODEL), lambda b, t: (b, 0, 0)),
                 _const_spec(nw.shape), _tok_spec(2 * LANES)]
                + [_const_spec(a.shape) for a in consts]
                + [_const_spec(w.shape) for w in weights])
    out_specs = [_tok_spec(o[0]) if len(o) == 2 else o[1] for o in outs]
    out_shape = [jax.ShapeDtypeStruct((bsz, seq, o[0]) if len(o) == 2 else o[0], o[-1]) for o in outs]
    return pl.pallas_call(
        kern,
        grid=(bsz, seq // TM_PROJ),
        in_specs=in_specs,
        out_specs=out_specs,
        out_shape=out_shape,
        compiler_params=pltpu.CompilerParams(
            dimension_semantics=("arbitrary", "arbitrary"), vmem_limit_bytes=VMEM_LIMIT),
        name=name,
    )(x, mod_l, nw, table, *consts, *weights)


def _gl_kernel(q_ref, k_ref, g_ref, v_ref, gate_ref, nw_ref, o_ref, st_ref, cum_ref, inter_ref, *, heads):
    @pl.when(pl.program_id(2) == 0)
    def _():
        st_ref[...] = jnp.zeros_like(st_ref)

    dk = LANES // heads
    row = lax.broadcasted_iota(jnp.int32, (CHUNK, CHUNK), 0)
    col = lax.broadcasted_iota(jnp.int32, (CHUNK, CHUNK), 1)
    causal = row >= col
    tri = jnp.where(causal, 1.0, 0.0).astype(BF16)
    lane_sub = lax.broadcasted_iota(jnp.int32, (SUB, LANES), 1)
    lane_chunk = lax.broadcasted_iota(jnp.int32, (CHUNK, LANES), 1)
    nw = nw_ref[...]
    n_sub = CHUNK // SUB
    zeros_sub = jnp.zeros((SUB, LANES), BF16)

    def own(t, lane, p):
        return t if heads == 1 else jnp.where((lane >= p * dk) & (lane < (p + 1) * dk), t, 0.0)

    chunks = [slice(c * CHUNK, (c + 1) * CHUNK) for c in range(q_ref.shape[0] // CHUNK)]
    heads_r = range(heads)

    def sub_block_decay(g):
        return functools.reduce(jnp.maximum, [-jnp.sum(g[SUB * a:SUB * (a + 1)], axis=0, keepdims=True)
                                              for a in range(n_sub)])

    cum = []
    spread = jnp.zeros((1, LANES), F32)
    for rows in chunks:
        g = g_ref[rows, :]
        spread = jnp.maximum(spread, sub_block_decay(g))
        g1 = g.astype(BF16)
        r1 = g - g1.astype(F32)
        g2 = r1.astype(BF16)
        g3 = (r1 - g2.astype(F32)).astype(BF16)
        cum.append(_dot(tri, g1) + _dot(tri, g2) + _dot(tri, g3))

    q_in, decay_tot, k_own, k_stack, q_stack = [], [], [], [], []
    for rows, b in zip(chunks, cum):
        q = q_ref[rows, :]
        k = k_ref[rows, :]
        btot = b[CHUNK - 1:CHUNK, :]
        q_in.append((q * jnp.exp(b)).astype(BF16))
        k_out = k * jnp.exp(btot - b)
        k_own.append([own(k_out, lane_chunk, p).astype(BF16) for p in heads_r])
        decay_tot.append(jnp.exp(btot))
        k_groups, q_groups = [], []
        for a in range(n_sub):
            hi = SUB * (a + 1)
            lvl = b[SUB * a - 1:SUB * a, :] if a else jnp.zeros((1, LANES), F32)
            kk = (k[:hi] * jnp.exp(jnp.minimum(lvl - b[:hi], EXP_CLAMP))).astype(BF16)
            if hi < CHUNK:
                kk = jnp.concatenate([kk, jnp.zeros((CHUNK - hi, LANES), BF16)], axis=0)
            k_groups.append(kk)
            q_groups.append(q[SUB * a:hi] * jnp.exp(b[SUB * a:hi] - lvl))
        k_stack.append(jnp.concatenate(k_groups, axis=1))
        per_head = []
        for p in heads_r:
            q_rows = []
            for a in range(n_sub):
                qa = own(q_groups[a], lane_sub, p).astype(BF16)
                q_rows.append(jnp.concatenate([zeros_sub] * a + [qa] + [zeros_sub] * (n_sub - 1 - a), axis=1))
            per_head.append(jnp.concatenate(q_rows, axis=0))
        q_stack.append(per_head)

    vals = [[v_ref[rows, p * LANES:(p + 1) * LANES] for p in heads_r] for rows in chunks]
    scores = [[lax.dot_general(q_stack[c][p], k_stack[c], NT_DIMS, preferred_element_type=F32)
               for p in heads_r] for c in range(len(chunks))]
    incr = [[lax.dot_general(vals[c][p], k_own[c][p], TN_DIMS, preferred_element_type=F32)
             for p in heads_r] for c in range(len(chunks))]

    outs = []
    for p in heads_r:
        st = st_ref[p]
        for c in range(len(chunks)):
            s = jnp.where(causal, scores[c][p], 0.0).astype(BF16)
            inter = lax.dot_general(q_in[c], st.astype(BF16), NT_DIMS, preferred_element_type=F32)
            inter_ref[p, chunks[c], :] = inter
            o = _dot(s, vals[c][p]) + inter
            outs.append((c, p, o))
            st = st * decay_tot[c] + incr[c][p]
        st_ref[p] = st

    def finish(rows, p, o):
        cols = slice(p * LANES, (p + 1) * LANES)
        o_ref[rows, cols] = (_head_norm(o, nw) * gate_ref[rows, cols]).astype(BF16)

    for c, p, o in outs:
        finish(chunks[c], p, o)

    for c, b in enumerate(cum):
        cum_ref[chunks[c], :] = b

    @pl.when(jnp.max(spread) > EXP_CLAMP)
    def _():
        def redo_chunk(c, carry):
            r0 = pl.multiple_of(c * CHUNK, CHUNK)
            rows = pl.ds(r0, CHUNK)

            @pl.when(jnp.max(sub_block_decay(g_ref[rows, :])) > EXP_CLAMP)
            def _():
                q = q_ref[rows, :]
                b = cum_ref[rows, :]

                def key_column(j, acc):
                    kj = k_ref[pl.ds(r0 + j, 1), :]
                    bj = cum_ref[pl.ds(r0 + j, 1), :]
                    t = q * kj * jnp.exp(jnp.minimum(b - bj, 0.0))
                    return tuple(jnp.where(col == j, jnp.sum(own(t, lane_chunk, p), axis=1, keepdims=True),
                                           acc[p]) for p in heads_r)

                exact = lax.fori_loop(0, CHUNK, key_column,
                                      tuple(jnp.zeros((CHUNK, CHUNK), F32) for _ in heads_r))
                for p in heads_r:
                    s = jnp.where(causal, exact[p], 0.0).astype(BF16)
                    finish(rows, p, _dot(s, v_ref[rows, p * LANES:(p + 1) * LANES]) + inter_ref[p, rows, :])
            return carry

        lax.fori_loop(0, len(chunks), redo_chunk, 0)


def _gated_linear(q, k, g, v, gate, nw, heads):
    bsz, seq, width = q.shape
    groups = width // LANES
    vw = heads * LANES
    qspec = pl.BlockSpec((None, TS_GL, LANES), lambda b, gi, s: (b, s, gi))
    vspec = pl.BlockSpec((None, TS_GL, vw), lambda b, gi, s: (b, s, gi))
    return pl.pallas_call(
        functools.partial(_gl_kernel, heads=heads),
        grid=(bsz, groups, seq // TS_GL),
        in_specs=[qspec, qspec, qspec, vspec, vspec, pl.BlockSpec((1, LANES), lambda b, gi, s: (0, 0))],
        out_specs=vspec,
        out_shape=jax.ShapeDtypeStruct((bsz, seq, groups * vw), BF16),
        scratch_shapes=[pltpu.VMEM((heads, LANES, LANES), F32), pltpu.VMEM((TS_GL, LANES), F32),
                        pltpu.VMEM((heads, TS_GL, LANES), F32)],
        compiler_params=pltpu.CompilerParams(
            dimension_semantics=("arbitrary", "arbitrary", "arbitrary"), vmem_limit_bytes=VMEM_LIMIT),
        name=f"gated_linear_{heads}",
    )(q, k, g, v, gate, nw)


def _swa_kernel(sink_ref, q_ref, cur_ref, prev_ref, o_ref):
    n = pl.program_id(1)
    kj = lax.broadcasted_iota(jnp.int32, (2 * WINDOW, WINDOW), 0)
    qi = lax.broadcasted_iota(jnp.int32, (2 * WINDOW, WINDOW), 1) + WINDOW
    rel = qi - kj
    band = (rel >= 0) & (rel < WINDOW)
    band_first = band & ((n > 0) | (kj >= WINDOW))
    lane = lax.broadcasted_iota(jnp.int32, (WINDOW, LANES), 1)
    low = lane < HEAD_DIM
    per_kv = N_SWA_HEADS // N_SWA_KV_HEADS
    chains = [(t, i, kv) for t in range(q_ref.shape[0] // WINDOW)
              for i in range(per_kv) for kv in range(N_SWA_KV_HEADS)]
    values, scores = {}, []
    for t, i, kv in chains:
        rows = slice(t * WINDOW, (t + 1) * WINDOW)
        if (i, kv) == (0, 0):
            cur = cur_ref[rows, :]
            prev = cur_ref[(t - 1) * WINDOW:t * WINDOW, :] if t else prev_ref[...]
            keys = jnp.concatenate([prev[:, :LANES], cur[:, :LANES]], axis=0)
            vals = jnp.concatenate([prev[:, LANES:], cur[:, LANES:]], axis=0)
            vlow = lax.broadcasted_iota(jnp.int32, vals.shape, 1) < HEAD_DIM
            one = jnp.ones_like(vals)
            values[t] = (jnp.where(vlow, vals, one), jnp.where(vlow, one, vals))
        qg = q_ref[rows, i * LANES:(i + 1) * LANES]
        qm = jnp.where(low if kv == 0 else ~low, qg, jnp.zeros_like(qg))
        scores.append(lax.dot_general(keys, qm, NT_DIMS, preferred_element_type=F32))
    probs, sink_terms = [], []
    for (t, i, kv), s in zip(chains, scores):
        sink = sink_ref[kv * per_kv + i] * LOG2_E
        s = jnp.where(band if t else band_first, s, -jnp.inf)
        m = jnp.maximum(jnp.max(s, axis=0, keepdims=True), sink)
        probs.append(jnp.exp2(s - m).astype(BF16))
        sink_terms.append(jnp.exp2(sink - m))
    outs = []
    for (t, _, kv), p, st in zip(chains, probs, sink_terms):
        pv = lax.dot_general(values[t][kv], p, TN_DIMS, preferred_element_type=F32)
        denom_row = HEAD_DIM if kv == 0 else 0
        outs.append(pv * (1.0 / (pv[denom_row:denom_row + 1, :] + st)))
    top = lax.broadcasted_iota(jnp.int32, (LANES, WINDOW), 0) < HEAD_DIM
    for idx in range(0, len(chains), N_SWA_KV_HEADS):
        t, i, _ = chains[idx]
        o_t = jnp.where(top, outs[idx], outs[idx + 1])
        o_ref[t * WINDOW:(t + 1) * WINDOW, i * LANES:(i + 1) * LANES] = o_t.T.astype(BF16)


def _sliding_window(sinks, q, kv):
    bsz, seq, _ = q.shape
    tq = SWA_BLOCKS * WINDOW
    return pl.pallas_call(
        _swa_kernel,
        grid=(bsz, seq // tq),
        in_specs=[
            pl.BlockSpec(memory_space=pltpu.SMEM),
            pl.BlockSpec((None, tq, MIX_HALF), lambda b, n: (b, n, 0)),
            pl.BlockSpec((None, tq, 2 * LANES), lambda b, n: (b, n, 0)),
            pl.BlockSpec((None, WINDOW, 2 * LANES), lambda b, n: (b, jnp.maximum(SWA_BLOCKS * n - 1, 0), 0)),
        ],
        out_specs=pl.BlockSpec((None, tq, MIX_HALF), lambda b, n: (b, n, 0)),
        out_shape=jax.ShapeDtypeStruct((bsz, seq, MIX_HALF), BF16),
        compiler_params=pltpu.CompilerParams(
            dimension_semantics=("arbitrary", "arbitrary"), vmem_limit_bytes=VMEM_LIMIT),
        name="sliding_window",
    )(sinks, q, kv, kv)


def _diff_kernel(lam_ref, q_ref, k_ref, vt_ref, nw_ref, o_ref, acc_ref, max_ref, s_ref, *, out_scale):
    qi = pl.program_id(1)
    tq = q_ref.shape[0]
    nh = q_ref.shape[1] // LANES
    lane = lax.broadcasted_iota(jnp.int32, (tq, LANES), 1)
    qs = []
    for h in range(nh):
        q = q_ref[:, h * LANES:(h + 1) * LANES]
        zero = jnp.zeros_like(q)
        qs += [jnp.where(lane < HEAD_DIM, q, zero), jnp.where(lane >= HEAD_DIM, q, zero)]
    key = lax.broadcasted_iota(jnp.int32, (tq, tq), 0)
    qry = lax.broadcasted_iota(jnp.int32, (tq, tq), 1)
    causal = key <= qry
    ones = jnp.ones((acc_ref.shape[1] - LANES, tq), BF16)
    acc_ref[...] = jnp.zeros_like(acc_ref)
    max_ref[...] = jnp.full(max_ref.shape, -jnp.inf, F32)

    def issue_scores(j, c):
        rows = pl.ds(pl.multiple_of(j * tq, tq), tq)
        kt = k_ref[rows, (c // 2) * LANES:(c // 2 + 1) * LANES]
        s_ref[c] = lax.dot_general(kt, qs[c], NT_DIMS, preferred_element_type=F32)

    def consume(j, c, masked):
        s = s_ref[c]
        if masked:
            s = jnp.where(causal, s, -jnp.inf)
        m = max_ref[c:c + 1, :]
        m_new = jnp.maximum(m, jnp.max(s, axis=0, keepdims=True))
        max_ref[c:c + 1, :] = m_new
        p = jnp.exp2(s - m_new).astype(BF16)
        vt1 = jnp.concatenate([vt_ref[j, c // 2], ones], axis=0)
        acc_ref[c] = jnp.exp2(m - m_new) * acc_ref[c] + _dot(vt1, p)

    for c in range(2 * nh):
        issue_scores(0, c)

    def body(j, carry):
        for c in range(2 * nh):
            consume(j, c, False)
            issue_scores(j + 1, c)
        return carry

    lax.fori_loop(0, qi, body, 0)
    for c in range(2 * nh):
        consume(qi, c, True)
    lam = lam_ref[0]
    nw = nw_ref[...]
    for h in range(nh):
        a1, a2 = acc_ref[2 * h], acc_ref[2 * h + 1]
        o_t = a1[:LANES] / a1[LANES:LANES + 1] - lam * (a2[:LANES] / a2[LANES:LANES + 1])
        o_ref[:, h * LANES:(h + 1) * LANES] = (_head_norm(o_t.T, nw) * out_scale).astype(BF16)


def _differential(lam, q, k, vt, nw, out_scale):
    bsz, seq, width = q.shape
    nh = width // LANES
    nt = seq // TQ_DIFF
    q_spec = pl.BlockSpec((None, TQ_DIFF, width), lambda b, i: (b, i, 0))
    return pl.pallas_call(
        functools.partial(_diff_kernel, out_scale=out_scale),
        grid=(bsz, nt),
        in_specs=[pl.BlockSpec(memory_space=pltpu.SMEM), q_spec,
                  pl.BlockSpec((None, seq, width), lambda b, i: (b, 0, 0)),
                  pl.BlockSpec((None, nt, nh, LANES, TQ_DIFF), lambda b, i: (b, 0, 0, 0, 0)),
                  pl.BlockSpec((1, LANES), lambda b, i: (0, 0))],
        out_specs=q_spec,
        out_shape=jax.ShapeDtypeStruct((bsz, seq, width), BF16),
        scratch_shapes=[pltpu.VMEM((2 * nh, LANES + ONES_ROWS, TQ_DIFF), F32),
                        pltpu.VMEM((2 * nh, TQ_DIFF), F32),
                        pltpu.VMEM((2 * nh, TQ_DIFF, TQ_DIFF), F32)],
        compiler_params=pltpu.CompilerParams(
            dimension_semantics=("arbitrary", "arbitrary"), vmem_limit_bytes=VMEM_LIMIT),
        name="differential",
    )(lam, q, k, vt, nw)


def _out_ffn_kernel(x_ref, oa_ref, ob_ref, mod_ref, nw_ref, fw_ref, woa_ref, wob_ref,
                    wi_ref, cw_ref, cb_ref, wo_ref,
                    y_ref, act_ref, carry_ref, *, final):
    t = pl.program_id(1)
    m = mod_ref[...]
    d = D_MODEL
    g1, sh2, sc2, g2 = m[:, 2 * d:3 * d], m[:, 3 * d:4 * d], m[:, 4 * d:5 * d], m[:, 5 * d:6 * d]
    mix = _dot(oa_ref[...], woa_ref[...]) + _dot(ob_ref[...], wob_ref[...])
    x1 = x_ref[...] + g1 * mix
    h = _modulated_norm(x1, nw_ref[...], sc2, sh2).astype(BF16)

    @pl.when(t == 0)
    def _():
        carry_ref[...] = jnp.zeros_like(carry_ref)

    tm = x1.shape[0]
    row = lax.broadcasted_iota(jnp.int32, (tm, FF_CHUNK), 0)
    for c in range(D_FF // FF_CHUNK):
        cols = slice(c * FF_CHUNK, (c + 1) * FF_CHUNK)
        a = _dot(h, wi_ref[:, cols])
        u = _dot(h, wi_ref[:, D_FF + c * FF_CHUNK:D_FF + (c + 1) * FF_CHUNK])
        tail = carry_ref[:, cols]
        cw = cw_ref[:, cols]
        prev1 = jnp.where(row < 1, tail[7:8, :], pltpu.roll(a, 1, 0))
        prev2 = jnp.where(row < 1, tail[6:7, :], jnp.where(row < 2, tail[7:8, :], pltpu.roll(a, 2, 0)))
        carry_ref[:, cols] = a[tm - 8:tm, :]
        conv = prev2 * cw[0:1, :] + prev1 * cw[1:2, :] + a * cw[2:3, :] + cb_ref[:, cols]
        act_ref[:, cols] = (_silu(conv) * u).astype(BF16)
    y = x1 + g2 * _dot(act_ref[...], wo_ref[...])
    if final:
        ms = jnp.mean(y * y, axis=-1, keepdims=True)
        y = (y * lax.rsqrt(ms + NORM_EPS)) * fw_ref[...]
    y_ref[...] = y


def _out_ffn(x, oa, ob, mod_l, nw, fw, woa, wob, wi, cw, cb, wo, final):
    bsz, seq, _ = x.shape

    def resident(shape):
        nd = len(shape)
        return pl.BlockSpec(shape, lambda b, t: (0,) * nd, pipeline_mode=pl.Buffered(1))

    return pl.pallas_call(
        functools.partial(_out_ffn_kernel, final=final),
        grid=(bsz, seq // TM_PROJ),
        in_specs=[_tok_spec(D_MODEL), _tok_spec(MIX_HALF), _tok_spec(MIX_HALF),
                  pl.BlockSpec((None, 1, 6 * D_MODEL), lambda b, t: (b, 0, 0)),
                  _const_spec(nw.shape), _const_spec(fw.shape),
                  resident(woa.shape), resident(wob.shape), resident(wi.shape),
                  _const_spec(cw.shape), _const_spec(cb.shape), resident(wo.shape)],
        out_specs=_tok_spec(D_MODEL),
        out_shape=jax.ShapeDtypeStruct(x.shape, F32),
        scratch_shapes=[pltpu.VMEM((TM_PROJ, D_FF), BF16), pltpu.VMEM((8, D_FF), F32)],
        compiler_params=pltpu.CompilerParams(
            dimension_semantics=("arbitrary", "arbitrary"), vmem_limit_bytes=VMEM_LIMIT),
        name="out_ffn",
    )(x, oa, ob, mod_l, nw, fw, woa, wob, wi, cw, cb, wo)


def _rope_lane_table(positions):
    half = ROT_DIMS // 2
    inv_freq = ROPE_THETA ** (-jnp.arange(0, ROT_DIMS, 2, dtype=F32) / ROT_DIMS)
    ang = inv_freq[:, None] * positions.astype(F32).reshape(1, -1)
    cs = jnp.concatenate([jnp.cos(ang), jnp.sin(ang)], axis=0)
    d = np.arange(LANES) % HEAD_DIM
    sel = np.zeros((2 * half, 2 * LANES), np.float32)
    base = np.zeros((1, 2 * LANES), np.float32)
    for lane in range(LANES):
        if d[lane] < ROT_DIMS:
            sel[d[lane] % half, lane] = 1.0
            sel[half + d[lane] % half, LANES + lane] = -1.0 if d[lane] < half else 1.0
        else:
            base[0, lane] = 1.0
    tab = lax.dot_general(cs, jnp.asarray(sel), TN_DIMS, precision=lax.Precision.HIGHEST) + base
    return tab.reshape(positions.shape + (2 * LANES,))


def _cols(w, start, width):
    return w[:, start:start + width].astype(BF16)


SWA_HEAD_ORDER = (0, 4, 1, 5, 2, 6, 3, 7)


def kernel(x, c, positions, mod_w, mod_b, norm_mix_w, norm_ffn_w, ev_w_in, gla_gate_w, gla_gate_b,
           gla_norm_w, swa_sinks, ev_w_out, od_w_in, diff_lambda, diff_norm_w, hgrn_lb_logits,
           hgrn_norm_w, od_w_out, ffn_w_in, ffn_conv_w, ffn_conv_b, ffn_w_out, final_norm_w):
    bsz, seq, _ = x.shape
    table = _rope_lane_table(positions)
    lbs = jnp.cumsum(jax.nn.softmax(hgrn_lb_logits.astype(F32), axis=0), axis=0)
    lbs = lbs - lbs[0:1]
    mod = _modulation(c, mod_w, mod_b).reshape(DEPTH, bsz, 1, 6 * D_MODEL)
    swa_cols = jnp.concatenate([jnp.arange(HEAD_DIM) + HEAD_DIM * h for h in SWA_HEAD_ORDER])
    fw = final_norm_w.reshape(1, D_MODEL)

    for l in range(DEPTH):
        j = l // 2
        nw_mix = norm_mix_w[l].reshape(1, D_MODEL)
        if l % 2 == 0:
            w = ev_w_in[j]
            w_sq = w[:, 1552:2064][:, swa_cols].astype(BF16)
            w_glr = jnp.pad(w[:, 1536:1552], ((0, 0), (0, LANES - GLA_GATE_RANK))).astype(BF16)
            gate_w = jnp.pad(gla_gate_w[j], ((0, LANES - GLA_GATE_RANK), (0, 0))).astype(BF16)
            weights = [_cols(w, 0, 1536), w_glr, gate_w, gla_gate_b[j].reshape(1, -1), w_sq, _cols(w, 2064, 256)]
            outs = [(256, F32), (256, F32), (512, BF16), (512, F32), (256, F32), (512, BF16), (256, BF16)]
            gq, gk, gv, gr, ga, sq, skv = _in_projection(
                _even_in_kernel, "even_in", x, mod[l], nw_mix, table, [], weights, outs)
            oa = _gated_linear(gq, gk, ga, gv, gr, gla_norm_w[j].reshape(1, LANES), heads=2)
            ob = _sliding_window(swa_sinks[j], sq, skv)
            w_out = ev_w_out[j]
            wob = w_out[MIX_HALF:][swa_cols].astype(BF16)
        else:
            w = od_w_in[j]
            lb = lbs[j].reshape(1, -1)
            consts = [jnp.log(lb), jnp.log1p(-lb), 1.0 - lb]
            weights = [w.astype(BF16)]
            tiles = TM_PROJ // TQ_DIFF
            n_dh = MIX_HALF // LANES
            vt_out = ((bsz, seq // TQ_DIFF, n_dh, LANES, TQ_DIFF),
                      pl.BlockSpec((None, tiles, n_dh, LANES, TQ_DIFF), lambda b, t: (b, t, 0, 0, 0)), BF16)
            outs = [(512, BF16), (512, BF16), vt_out, (512, F32), (512, F32), (512, F32),
                    (512, BF16), (512, F32)]
            dq, dk, dvt, hq, hk, hf, hi, hg = _in_projection(
                _odd_in_kernel, "odd_in", x, mod[l], nw_mix, table, consts, weights, outs)
            lam_init = 0.8 - 0.6 * math.exp(-0.3 * l)
            lv = diff_lambda[j].astype(F32)
            lam = jnp.exp(jnp.sum(lv[0] * lv[1])) - jnp.exp(jnp.sum(lv[2] * lv[3])) + lam_init
            oa = _differential(lam.reshape(1), dq, dk, dvt, diff_norm_w[j].reshape(1, LANES), 1.0 - lam_init)
            ob = _gated_linear(hq, hk, hf, hi, hg, hgrn_norm_w[j].reshape(1, LANES), heads=1)
            w_out = od_w_out[j]
            wob = w_out[MIX_HALF:].astype(BF16)
        woa = w_out[:MIX_HALF].astype(BF16)
        x = _out_ffn(x, oa, ob, mod[l], norm_ffn_w[l].reshape(1, D_MODEL), fw, woa, wob,
                     ffn_w_in[l].astype(BF16), ffn_conv_w[l], ffn_conv_b[l].reshape(1, D_FF),
                     ffn_w_out[l].astype(BF16), final=(l == DEPTH - 1))
    return x
```

```python
import functools
import math

import jax
import numpy as np
import jax.numpy as jnp
from jax import lax
from jax.experimental import pallas as pl
from jax.experimental.pallas import tpu as pltpu

F32 = jnp.float32
BF16 = jnp.bfloat16

D_MODEL = 1024
DEPTH = 4
HEAD_DIM = 64
ROT_DIMS = HEAD_DIM // 4
ROPE_THETA = 500000.0
GLA_GATE_RANK = 16
GLA_GATE_NORMALIZER = 16.0
N_SWA_HEADS = 8
N_SWA_KV_HEADS = 2
WINDOW = 128
HGRN_EXPAND = 128
D_FF = 2816
MIX_HALF = D_MODEL // 2
NORM_EPS = 1e-6
LOG2_E = math.log2(math.e)

LANES = 128
CHUNK = 64
SUB = 32
EXP_CLAMP = 60.0
TM_PROJ = 512
TM_FFN = 1024
TS_GL = 2048
TQ_DIFF = 512
SWA_BLOCKS = 4
ONES_ROWS = 16
FF_CHUNK = 256
VMEM_LIMIT = 56 * 1024 * 1024

NT_DIMS = (((1,), (1,)), ((), ()))
TN_DIMS = (((0,), (0,)), ((), ()))


def _sigmoid(z):
    return 1.0 / (1.0 + jnp.exp(-z))


def _silu(z):
    return z * _sigmoid(z)


def _log_sigmoid(z):
    return jnp.minimum(z, 0.0) - jnp.log(1.0 + jnp.exp(-jnp.abs(z)))


def _dot(a, b):
    return jnp.dot(a, b, preferred_element_type=F32)


def _modulated_norm(x, nw, sc, sh):
    ms = jnp.mean(x * x, axis=-1, keepdims=True)
    return (x * lax.rsqrt(ms + NORM_EPS)) * (nw * (1.0 + sc)) + sh


def _head_norm(o, nw):
    ms = jnp.mean(o * o, axis=-1, keepdims=True)
    return (o * lax.rsqrt(ms + NORM_EPS)) * nw


def _rope_tables(tab):
    c, s = tab[:, :LANES], tab[:, LANES:]
    lane = lax.broadcasted_iota(jnp.int32, c.shape, 1)
    first_half = (lane & (HEAD_DIM - 1)) < ROT_DIMS // 2
    return c, jnp.where(first_half, s, 0.0), jnp.where(first_half, 0.0, s)


def _rope(x, tables):
    n = x.shape[-1] // LANES
    c, s_up, s_down = (jnp.concatenate([t] * n, axis=-1) if n > 1 else t for t in tables)
    half = ROT_DIMS // 2
    up = pltpu.roll(x, x.shape[-1] - half, 1)
    down = pltpu.roll(x, half, 1)
    return x * c + up * s_up + down * s_down


def _mod_kernel(c_ref, w_ref, b_ref, o_ref):
    ca = _silu(c_ref[...]).astype(BF16)
    o_ref[0] = _dot(ca, w_ref[0].astype(BF16)) + b_ref[0]


def _modulation(c, mod_w, mod_b):
    bsz = c.shape[0]
    tn = 1024
    n_out = mod_w.shape[-1]
    return pl.pallas_call(
        _mod_kernel,
        grid=(DEPTH, n_out // tn),
        in_specs=[
            pl.BlockSpec((bsz, D_MODEL), lambda l, j: (0, 0)),
            pl.BlockSpec((1, D_MODEL, tn), lambda l, j: (l, 0, j)),
            pl.BlockSpec((1, 1, tn), lambda l, j: (l, 0, j)),
        ],
        out_specs=pl.BlockSpec((1, bsz, tn), lambda l, j: (l, 0, j)),
        out_shape=jax.ShapeDtypeStruct((DEPTH, bsz, n_out), F32),
        name="modulation",
    )(c, mod_w, mod_b.reshape(DEPTH, 1, n_out))


def _even_in_kernel(x_ref, mod_ref, nw_ref, tab_ref,
                    wg_ref, wl_ref, gw_ref, gb_ref, wsq_ref, wskv_ref,
                    gq_ref, gk_ref, gv_ref, gr_ref, ga_ref, sq_ref, skv_ref):
    m = mod_ref[...]
    h = _modulated_norm(x_ref[...], nw_ref[...], m[:, D_MODEL:2 * D_MODEL], m[:, 0:D_MODEL]).astype(BF16)
    glr = _dot(h, wl_ref[...]).astype(BF16)
    gq = _dot(h, wg_ref[:, 0:256])
    gk = _dot(h, wg_ref[:, 256:512])
    gv = _dot(h, wg_ref[:, 512:1024])
    gr = _dot(h, wg_ref[:, 1024:1536])
    sq = _dot(h, wsq_ref[...])
    skv = _dot(h, wskv_ref[...])
    gate = _dot(glr, gw_ref[...]) + gb_ref[...]
    gq_ref[...] = gq * (HEAD_DIM ** -0.5)
    gk_ref[...] = gk
    gv_ref[...] = gv.astype(BF16)
    gr_ref[...] = _silu(gr)
    ga_ref[...] = _log_sigmoid(gate) / GLA_GATE_NORMALIZER
    tab = _rope_tables(tab_ref[...])
    sq_ref[...] = (_rope(sq, tab) * (LOG2_E * HEAD_DIM ** -0.5)).astype(BF16)
    sk = _rope(skv[:, :LANES], tab)
    skv_ref[...] = jnp.concatenate([sk, skv[:, LANES:]], axis=-1).astype(BF16)


def _odd_in_kernel(x_ref, mod_ref, nw_ref, tab_ref, llb_ref, l1m_ref, oml_ref, w_ref,
                   dq_ref, dk_ref, dvt_ref, hq_ref, hk_ref, hf_ref, hi_ref, hg_ref):
    wdq_ref, wdk_ref, wdv_ref, whq_ref, whf_ref, whi_ref, whg_ref = (
        w_ref.at[:, i * MIX_HALF:(i + 1) * MIX_HALF] for i in range(7))
    m = mod_ref[...]
    h = _modulated_norm(x_ref[...], nw_ref[...], m[:, D_MODEL:2 * D_MODEL], m[:, 0:D_MODEL]).astype(BF16)
    tab = _rope_tables(tab_ref[...])
    z, hq, hg, dq, dk, dv, hi = (_dot(h, w[...]) for w in
                                 (whf_ref, whq_ref, whg_ref, wdq_ref, wdk_ref, wdv_ref, whi_ref))
    ez = jnp.exp(-jnp.abs(z))
    a1 = llb_ref[...]
    a2 = l1m_ref[...] + (jnp.minimum(z, 0.0) - jnp.log(1.0 + ez))
    hf_ref[...] = jnp.maximum(a1, a2) + jnp.log(1.0 + jnp.exp(-jnp.abs(a1 - a2)))
    hk_ref[...] = oml_ref[...] * (jnp.where(z >= 0.0, ez, 1.0) / (1.0 + ez))
    hq_ref[...] = _silu(hq) * (HGRN_EXPAND ** -0.5)
    hg_ref[...] = _silu(hg)
    dq_ref[...] = (_rope(dq, tab) * (LOG2_E * HEAD_DIM ** -0.5)).astype(BF16)
    dk_ref[...] = _rope(dk, tab).astype(BF16)
    for t in range(dvt_ref.shape[0]):
        for hd in range(dvt_ref.shape[1]):
            tile = dv[t * TQ_DIFF:(t + 1) * TQ_DIFF, hd * LANES:(hd + 1) * LANES]
            dvt_ref[t, hd] = tile.T.astype(BF16)
    hi_ref[...] = hi.astype(BF16)


def _tok_spec(width):
    return pl.BlockSpec((None, TM_PROJ, width), lambda b, t: (b, t, 0))


def _const_spec(shape):
    nd = len(shape)
    return pl.BlockSpec(shape, lambda b, t: (0,) * nd)


def _in_projection(kern, name, x, mod_l, nw, table, consts, weights, outs):
    bsz, seq, _ = x.shape
    in_specs = ([_tok_spec(D_MODEL), pl.BlockSpec((None, 1, 6 * D_MODEL), lambda b, t: (b, 0, 0)),
                 _const_spec(nw.shape), _tok_spec(2 * LANES)]
                + [_const_spec(a.shape) for a in consts]
                + [_const_spec(w.shape) for w in weights])
    out_specs = [_tok_spec(o[0]) if len(o) == 2 else o[1] for o in outs]
    out_shape = [jax.ShapeDtypeStruct((bsz, seq, o[0]) if len(o) == 2 else o[0], o[-1]) for o in outs]
    return pl.pallas_call(
        kern,
        grid=(bsz, seq // TM_PROJ),
        in_specs=in_specs,
        out_specs=out_specs,
        out_shape=out_shape,
        compiler_params=pltpu.CompilerParams(
            dimension_semantics=("arbitrary", "arbitrary"), vmem_limit_bytes=VMEM_LIMIT),
        name=name,
    )(x, mod_l, nw, table, *consts, *weights)


def _gl_kernel(q_ref, k_ref, g_ref, v_ref, gate_ref, nw_ref, o_ref, st_ref, cum_ref, inter_ref, *, heads):
    @pl.when(pl.program_id(2) == 0)
    def _():
        st_ref[...] = jnp.zeros_like(st_ref)

    dk = LANES // heads
    row = lax.broadcasted_iota(jnp.int32, (CHUNK, CHUNK), 0)
    col = lax.broadcasted_iota(jnp.int32, (CHUNK, CHUNK), 1)
    causal = row >= col
    tri = jnp.where(causal, 1.0, 0.0).astype(BF16)
    lane_sub = lax.broadcasted_iota(jnp.int32, (SUB, LANES), 1)
    lane_chunk = lax.broadcasted_iota(jnp.int32, (CHUNK, LANES), 1)
    nw = nw_ref[...]
    n_sub = CHUNK // SUB
    zeros_sub = jnp.zeros((SUB, LANES), BF16)

    def own(t, lane, p):
        return t if heads == 1 else jnp.where((lane >= p * dk) & (lane < (p + 1) * dk), t, 0.0)

    chunks = [slice(c * CHUNK, (c + 1) * CHUNK) for c in range(q_ref.shape[0] // CHUNK)]
    heads_r = range(heads)

    def sub_block_decay(g):
        return functools.reduce(jnp.maximum, [-jnp.sum(g[SUB * a:SUB * (a + 1)], axis=0, keepdims=True)
                                              for a in range(n_sub)])

    cum = []
    spread = jnp.zeros((1, LANES), F32)
    for rows in chunks:
        g = g_ref[rows, :]
        spread = jnp.maximum(spread, sub_block_decay(g))
        g1 = g.astype(BF16)
        r1 = g - g1.astype(F32)
        g2 = r1.astype(BF16)
        g3 = (r1 - g2.astype(F32)).astype(BF16)
        cum.append(_dot(tri, g1) + _dot(tri, g2) + _dot(tri, g3))

    q_in, decay_tot, k_own, k_stack, q_stack = [], [], [], [], []
    for rows, b in zip(chunks, cum):
        q = q_ref[rows, :]
        k = k_ref[rows, :]
        btot = b[CHUNK - 1:CHUNK, :]
        q_in.append((q * jnp.exp(b)).astype(BF16))
        k_out = k * jnp.exp(btot - b)
        k_own.append([own(k_out, lane_chunk, p).astype(BF16) for p in heads_r])
        decay_tot.append(jnp.exp(btot))
        k_groups, q_groups = [], []
        for a in range(n_sub):
            hi = SUB * (a + 1)
            lvl = b[SUB * a - 1:SUB * a, :] if a else jnp.zeros((1, LANES), F32)
            kk = (k[:hi] * jnp.exp(jnp.minimum(lvl - b[:hi], EXP_CLAMP))).astype(BF16)
            if hi < CHUNK:
                kk = jnp.concatenate([kk, jnp.zeros((CHUNK - hi, LANES), BF16)], axis=0)
            k_groups.append(kk)
            q_groups.append(q[SUB * a:hi] * jnp.exp(b[SUB * a:hi] - lvl))
        k_stack.append(jnp.concatenate(k_groups, axis=1))
        per_head = []
        for p in heads_r:
            q_rows = []
            for a in range(n_sub):
                qa = own(q_groups[a], lane_sub, p).astype(BF16)
                q_rows.append(jnp.concatenate([zeros_sub] * a + [qa] + [zeros_sub] * (n_sub - 1 - a), axis=1))
            per_head.append(jnp.concatenate(q_rows, axis=0))
        q_stack.append(per_head)

    vals = [[v_ref[rows, p * LANES:(p + 1) * LANES] for p in heads_r] for rows in chunks]
    scores = [[lax.dot_general(q_stack[c][p], k_stack[c], NT_DIMS, preferred_element_type=F32)
               for p in heads_r] for c in range(len(chunks))]
    incr = [[lax.dot_general(vals[c][p], k_own[c][p], TN_DIMS, preferred_element_type=F32)
             for p in heads_r] for c in range(len(chunks))]

    outs = []
    for p in heads_r:
        st = st_ref[p]
        for c in range(len(chunks)):
            s = jnp.where(causal, scores[c][p], 0.0).astype(BF16)
            inter = lax.dot_general(q_in[c], st.astype(BF16), NT_DIMS, preferred_element_type=F32)
            inter_ref[p, chunks[c], :] = inter
            o = _dot(s, vals[c][p]) + inter
            outs.append((c, p, o))
            st = st * decay_tot[c] + incr[c][p]
        st_ref[p] = st

    def finish(rows, p, o):
        cols = slice(p * LANES, (p + 1) * LANES)
        o_ref[rows, cols] = (_head_norm(o, nw) * gate_ref[rows, cols]).astype(BF16)

    for c, p, o in outs:
        finish(chunks[c], p, o)

    for c, b in enumerate(cum):
        cum_ref[chunks[c], :] = b

    @pl.when(jnp.max(spread) > EXP_CLAMP)
    def _():
        def redo_chunk(c, carry):
            r0 = pl.multiple_of(c * CHUNK, CHUNK)
            rows = pl.ds(r0, CHUNK)

            @pl.when(jnp.max(sub_block_decay(g_ref[rows, :])) > EXP_CLAMP)
            def _():
                q = q_ref[rows, :]
                b = cum_ref[rows, :]

                def key_column(j, acc):
                    kj = k_ref[pl.ds(r0 + j, 1), :]
                    bj = cum_ref[pl.ds(r0 + j, 1), :]
                    t = q * kj * jnp.exp(jnp.minimum(b - bj, 0.0))
                    return tuple(jnp.where(col == j, jnp.sum(own(t, lane_chunk, p), axis=1, keepdims=True),
                                           acc[p]) for p in heads_r)

                exact = lax.fori_loop(0, CHUNK, key_column,
                                      tuple(jnp.zeros((CHUNK, CHUNK), F32) for _ in heads_r))
                for p in heads_r:
                    s = jnp.where(causal, exact[p], 0.0).astype(BF16)
                    finish(rows, p, _dot(s, v_ref[rows, p * LANES:(p + 1) * LANES]) + inter_ref[p, rows, :])
            return carry

        lax.fori_loop(0, len(chunks), redo_chunk, 0)


def _gated_linear(q, k, g, v, gate, nw, heads):
    bsz, seq, width = q.shape
    groups = width // LANES
    vw = heads * LANES
    qspec = pl.BlockSpec((None, TS_GL, LANES), lambda b, gi, s: (b, s, gi))
    vspec = pl.BlockSpec((None, TS_GL, vw), lambda b, gi, s: (b, s, gi))
    return pl.pallas_call(
        functools.partial(_gl_kernel, heads=heads),
        grid=(bsz, groups, seq // TS_GL),
        in_specs=[qspec, qspec, qspec, vspec, vspec, pl.BlockSpec((1, LANES), lambda b, gi, s: (0, 0))],
        out_specs=vspec,
        out_shape=jax.ShapeDtypeStruct((bsz, seq, groups * vw), BF16),
        scratch_shapes=[pltpu.VMEM((heads, LANES, LANES), F32), pltpu.VMEM((TS_GL, LANES), F32),
                        pltpu.VMEM((heads, TS_GL, LANES), F32)],
        compiler_params=pltpu.CompilerParams(
            dimension_semantics=("arbitrary", "arbitrary", "arbitrary"), vmem_limit_bytes=VMEM_LIMIT),
        name=f"gated_linear_{heads}",
    )(q, k, g, v, gate, nw)


def _swa_kernel(sink_ref, q_ref, cur_ref, prev_ref, o_ref):
    n = pl.program_id(1)
    kj = lax.broadcasted_iota(jnp.int32, (2 * WINDOW, WINDOW), 0)
    qi = lax.broadcasted_iota(jnp.int32, (2 * WINDOW, WINDOW), 1) + WINDOW
    rel = qi - kj
    band = (rel >= 0) & (rel < WINDOW)
    band_first = band & ((n > 0) | (kj >= WINDOW))
    lane = lax.broadcasted_iota(jnp.int32, (WINDOW, LANES), 1)
    low = lane < HEAD_DIM
    per_kv = N_SWA_HEADS // N_SWA_KV_HEADS
    chains = [(t, i, kv) for t in range(q_ref.shape[0] // WINDOW)
              for i in range(per_kv) for kv in range(N_SWA_KV_HEADS)]
    values, scores = {}, {}
    for t in range(q_ref.shape[0] // WINDOW):
        rows = slice(t * WINDOW, (t + 1) * WINDOW)
        cur = cur_ref[rows, :]
        prev = cur_ref[(t - 1) * WINDOW:t * WINDOW, :] if t else prev_ref[...]
        keys = jnp.concatenate([prev[:, :LANES], cur[:, :LANES]], axis=0)
        vals = jnp.concatenate([prev[:, LANES:], cur[:, LANES:]], axis=0)
        vlow = lax.broadcasted_iota(jnp.int32, vals.shape, 1) < HEAD_DIM
        one = jnp.ones_like(vals)
        values[t] = (jnp.where(vlow, vals, one), jnp.where(vlow, one, vals))
        for i in range(per_kv):
            qg = q_ref[rows, i * LANES:(i + 1) * LANES]
            zero = jnp.zeros_like(qg)
            both = jnp.concatenate([jnp.where(low, qg, zero), jnp.where(low, zero, qg)], axis=0)
            s2 = lax.dot_general(keys, both, NT_DIMS, preferred_element_type=F32)
            scores[t, i, 0], scores[t, i, 1] = s2[:, :WINDOW], s2[:, WINDOW:]
    probs, sink_terms = {}, {}
    for t, i, kv in chains:
        sink = sink_ref[kv * per_kv + i] * LOG2_E
        s = jnp.where(band if t else band_first, scores[t, i, kv], -jnp.inf)
        m = jnp.maximum(jnp.max(s, axis=0, keepdims=True), sink)
        probs[t, i, kv] = jnp.exp2(s - m).astype(BF16)
        sink_terms[t, i, kv] = jnp.exp2(sink - m)
    outs = {}
    for t, i, kv in chains:
        if i % 2:
            continue
        pair = jnp.concatenate([probs[t, i, kv], probs[t, i + 1, kv]], axis=1)
        pv2 = lax.dot_general(values[t][kv], pair, TN_DIMS, preferred_element_type=F32)
        denom_row = HEAD_DIM if kv == 0 else 0
        for off, ii in ((0, i), (WINDOW, i + 1)):
            pv = pv2[:, off:off + WINDOW]
            outs[t, ii, kv] = pv * (1.0 / (pv[denom_row:denom_row + 1, :] + sink_terms[t, ii, kv]))
    top = lax.broadcasted_iota(jnp.int32, (LANES, WINDOW), 0) < HEAD_DIM
    for t, i, kv in chains:
        if kv == 0:
            o_t = jnp.where(top, outs[t, i, 0], outs[t, i, 1])
            o_ref[t * WINDOW:(t + 1) * WINDOW, i * LANES:(i + 1) * LANES] = o_t.T.astype(BF16)


def _sliding_window(sinks, q, kv):
    bsz, seq, _ = q.shape
    tq = SWA_BLOCKS * WINDOW
    return pl.pallas_call(
        _swa_kernel,
        grid=(bsz, seq // tq),
        in_specs=[
            pl.BlockSpec(memory_space=pltpu.SMEM),
            pl.BlockSpec((None, tq, MIX_HALF), lambda b, n: (b, n, 0)),
            pl.BlockSpec((None, tq, 2 * LANES), lambda b, n: (b, n, 0)),
            pl.BlockSpec((None, WINDOW, 2 * LANES), lambda b, n: (b, jnp.maximum(SWA_BLOCKS * n - 1, 0), 0)),
        ],
        out_specs=pl.BlockSpec((None, tq, MIX_HALF), lambda b, n: (b, n, 0)),
        out_shape=jax.ShapeDtypeStruct((bsz, seq, MIX_HALF), BF16),
        compiler_params=pltpu.CompilerParams(
            dimension_semantics=("arbitrary", "arbitrary"), vmem_limit_bytes=VMEM_LIMIT),
        name="sliding_window",
    )(sinks, q, kv, kv)


def _diff_kernel(lam_ref, q_ref, k_ref, vt_ref, nw_ref, o_ref, acc_ref, max_ref, s_ref, *, out_scale):
    qi = pl.program_id(1)
    tq = q_ref.shape[0]
    nh = q_ref.shape[1] // LANES
    lane = lax.broadcasted_iota(jnp.int32, (tq, LANES), 1)
    qs = []
    for h in range(nh):
        q = q_ref[:, h * LANES:(h + 1) * LANES]
        zero = jnp.zeros_like(q)
        qs += [jnp.where(lane < HEAD_DIM, q, zero), jnp.where(lane >= HEAD_DIM, q, zero)]
    key = lax.broadcasted_iota(jnp.int32, (tq, tq), 0)
    qry = lax.broadcasted_iota(jnp.int32, (tq, tq), 1)
    causal = key <= qry
    ones = jnp.ones((acc_ref.shape[1] - LANES, tq), BF16)
    acc_ref[...] = jnp.zeros_like(acc_ref)
    max_ref[...] = jnp.full(max_ref.shape, -jnp.inf, F32)

    def issue_scores(j, c):
        rows = pl.ds(pl.multiple_of(j * tq, tq), tq)
        kt = k_ref[rows, (c // 2) * LANES:(c // 2 + 1) * LANES]
        s_ref[c] = lax.dot_general(kt, qs[c], NT_DIMS, preferred_element_type=F32)

    def consume(j, c, masked):
        s = s_ref[c]
        if masked:
            s = jnp.where(causal, s, -jnp.inf)
        m = max_ref[c:c + 1, :]
        m_new = jnp.maximum(m, jnp.max(s, axis=0, keepdims=True))
        max_ref[c:c + 1, :] = m_new
        p = jnp.exp2(s - m_new).astype(BF16)
        vt1 = jnp.concatenate([vt_ref[j, c // 2], ones], axis=0)
        acc_ref[c] = jnp.exp2(m - m_new) * acc_ref[c] + _dot(vt1, p)

    for c in range(2 * nh):
        issue_scores(0, c)

    def body(j, carry):
        for c in range(2 * nh):
            consume(j, c, False)
            issue_scores(j + 1, c)
        return carry

    lax.fori_loop(0, qi, body, 0)
    for c in range(2 * nh):
        consume(qi, c, True)
    lam = lam_ref[0]
    nw = nw_ref[...]
    for h in range(nh):
        a1, a2 = acc_ref[2 * h], acc_ref[2 * h + 1]
        o_t = a1[:LANES] / a1[LANES:LANES + 1] - lam * (a2[:LANES] / a2[LANES:LANES + 1])
        o_ref[:, h * LANES:(h + 1) * LANES] = (_head_norm(o_t.T, nw) * out_scale).astype(BF16)


def _differential(lam, q, k, vt, nw, out_scale):
    bsz, seq, width = q.shape
    nh = width // LANES
    nt = seq // TQ_DIFF
    q_spec = pl.BlockSpec((None, TQ_DIFF, width), lambda b, i: (b, i, 0))
    return pl.pallas_call(
        functools.partial(_diff_kernel, out_scale=out_scale),
        grid=(bsz, nt),
        in_specs=[pl.BlockSpec(memory_space=pltpu.SMEM), q_spec,
                  pl.BlockSpec((None, seq, width), lambda b, i: (b, 0, 0)),
                  pl.BlockSpec((None, nt, nh, LANES, TQ_DIFF), lambda b, i: (b, 0, 0, 0, 0)),
                  pl.BlockSpec((1, LANES), lambda b, i: (0, 0))],
        out_specs=q_spec,
        out_shape=jax.ShapeDtypeStruct((bsz, seq, width), BF16),
        scratch_shapes=[pltpu.VMEM((2 * nh, LANES + ONES_ROWS, TQ_DIFF), F32),
                        pltpu.VMEM((2 * nh, TQ_DIFF), F32),
                        pltpu.VMEM((2 * nh, TQ_DIFF, TQ_DIFF), F32)],
        compiler_params=pltpu.CompilerParams(
            dimension_semantics=("arbitrary", "arbitrary"), vmem_limit_bytes=VMEM_LIMIT),
        name="differential",
    )(lam, q, k, vt, nw)


def _out_ffn_kernel(x_ref, oa_ref, ob_ref, mod_ref, nw_ref, fw_ref, woa_ref, wob_ref,
                    wi_ref, cw_ref, cb_ref, wo_ref,
                    y_ref, act_ref, carry_ref, *, final):
    t = pl.program_id(1)
    m = mod_ref[...]
    d = D_MODEL
    g1, sh2, sc2, g2 = m[:, 2 * d:3 * d], m[:, 3 * d:4 * d], m[:, 4 * d:5 * d], m[:, 5 * d:6 * d]
    @pl.when(t == 0)
    def _():
        carry_ref[...] = jnp.zeros_like(carry_ref)

    tm = act_ref.shape[0]
    row = lax.broadcasted_iota(jnp.int32, (tm, FF_CHUNK), 0)
    for r in range(x_ref.shape[0] // tm):
        rows = slice(r * tm, (r + 1) * tm)
        mix = _dot(oa_ref[rows, :], woa_ref[...]) + _dot(ob_ref[rows, :], wob_ref[...])
        x1 = x_ref[rows, :] + g1 * mix
        h = _modulated_norm(x1, nw_ref[...], sc2, sh2).astype(BF16)
        for c in range(D_FF // FF_CHUNK):
            cols = slice(c * FF_CHUNK, (c + 1) * FF_CHUNK)
            a = _dot(h, wi_ref[:, cols])
            u = _dot(h, wi_ref[:, D_FF + c * FF_CHUNK:D_FF + (c + 1) * FF_CHUNK])
            tail = carry_ref[:, cols]
            cw = cw_ref[:, cols]
            prev1 = jnp.where(row < 1, tail[7:8, :], pltpu.roll(a, 1, 0))
            prev2 = jnp.where(row < 1, tail[6:7, :], jnp.where(row < 2, tail[7:8, :], pltpu.roll(a, 2, 0)))
            carry_ref[:, cols] = a[tm - 8:tm, :]
            conv = prev2 * cw[0:1, :] + prev1 * cw[1:2, :] + a * cw[2:3, :] + cb_ref[:, cols]
            act_ref[:, cols] = (_silu(conv) * u).astype(BF16)
        y = x1 + g2 * _dot(act_ref[...], wo_ref[...])
        if final:
            ms = jnp.mean(y * y, axis=-1, keepdims=True)
            y = (y * lax.rsqrt(ms + NORM_EPS)) * fw_ref[...]
        y_ref[rows, :] = y


def _out_ffn(x, oa, ob, mod_l, nw, fw, woa, wob, wi_all, cw, cb, wo_all, layer, final):
    bsz, seq, _ = x.shape

    def resident(shape):
        nd = len(shape)
        return pl.BlockSpec(shape, lambda b, t: (0,) * nd, pipeline_mode=pl.Buffered(1))

    def layer_resident(shape):
        return pl.BlockSpec((None,) + shape[1:], lambda b, t: (layer, 0, 0), pipeline_mode=pl.Buffered(1))

    def tok(width):
        return pl.BlockSpec((None, TM_FFN, width), lambda b, t: (b, t, 0))

    return pl.pallas_call(
        functools.partial(_out_ffn_kernel, final=final),
        grid=(bsz, seq // TM_FFN),
        in_specs=[tok(D_MODEL), tok(MIX_HALF), tok(MIX_HALF),
                  pl.BlockSpec((None, 1, 6 * D_MODEL), lambda b, t: (b, 0, 0)),
                  _const_spec(nw.shape), _const_spec(fw.shape),
                  resident(woa.shape), resident(wob.shape), layer_resident(wi_all.shape),
                  _const_spec(cw.shape), _const_spec(cb.shape), layer_resident(wo_all.shape)],
        out_specs=tok(D_MODEL),
        out_shape=jax.ShapeDtypeStruct(x.shape, F32),
        scratch_shapes=[pltpu.VMEM((TM_PROJ, D_FF), BF16), pltpu.VMEM((8, D_FF), F32)],
        compiler_params=pltpu.CompilerParams(
            dimension_semantics=("arbitrary", "arbitrary"), vmem_limit_bytes=VMEM_LIMIT),
        name="out_ffn",
    )(x, oa, ob, mod_l, nw, fw, woa, wob, wi_all, cw, cb, wo_all)


def _rope_lane_table(positions):
    half = ROT_DIMS // 2
    inv_freq = ROPE_THETA ** (-jnp.arange(0, ROT_DIMS, 2, dtype=F32) / ROT_DIMS)
    ang = inv_freq[:, None] * positions.astype(F32).reshape(1, -1)
    cs = jnp.concatenate([jnp.cos(ang), jnp.sin(ang)], axis=0)
    d = np.arange(LANES) % HEAD_DIM
    sel = np.zeros((2 * half, 2 * LANES), np.float32)
    base = np.zeros((1, 2 * LANES), np.float32)
    for lane in range(LANES):
        if d[lane] < ROT_DIMS:
            sel[d[lane] % half, lane] = 1.0
            sel[half + d[lane] % half, LANES + lane] = -1.0 if d[lane] < half else 1.0
        else:
            base[0, lane] = 1.0
    tab = lax.dot_general(cs, jnp.asarray(sel), TN_DIMS, precision=lax.Precision.HIGHEST) + base
    return tab.reshape(positions.shape + (2 * LANES,))


def _cols(w, start, width):
    return w[:, start:start + width].astype(BF16)


SWA_HEAD_ORDER = (0, 4, 1, 5, 2, 6, 3, 7)


def kernel(x, c, positions, mod_w, mod_b, norm_mix_w, norm_ffn_w, ev_w_in, gla_gate_w, gla_gate_b,
           gla_norm_w, swa_sinks, ev_w_out, od_w_in, diff_lambda, diff_norm_w, hgrn_lb_logits,
           hgrn_norm_w, od_w_out, ffn_w_in, ffn_conv_w, ffn_conv_b, ffn_w_out, final_norm_w):
    bsz, seq, _ = x.shape
    table = _rope_lane_table(positions)
    lbs = jnp.cumsum(jax.nn.softmax(hgrn_lb_logits.astype(F32), axis=0), axis=0)
    lbs = lbs - lbs[0:1]
    mod = _modulation(c, mod_w, mod_b).reshape(DEPTH, bsz, 1, 6 * D_MODEL)
    swa_cols = jnp.concatenate([jnp.arange(HEAD_DIM) + HEAD_DIM * h for h in SWA_HEAD_ORDER])
    fw = final_norm_w.reshape(1, D_MODEL)
    ffn_wi, ffn_wo = ffn_w_in.astype(BF16), ffn_w_out.astype(BF16)

    for l in range(DEPTH):
        j = l // 2
        nw_mix = norm_mix_w[l].reshape(1, D_MODEL)
        if l % 2 == 0:
            w = ev_w_in[j]
            w_sq = w[:, 1552:2064][:, swa_cols].astype(BF16)
            w_glr = jnp.pad(w[:, 1536:1552], ((0, 0), (0, LANES - GLA_GATE_RANK))).astype(BF16)
            gate_w = jnp.pad(gla_gate_w[j], ((0, LANES - GLA_GATE_RANK), (0, 0))).astype(BF16)
            weights = [_cols(w, 0, 1536), w_glr, gate_w, gla_gate_b[j].reshape(1, -1), w_sq, _cols(w, 2064, 256)]
            outs = [(256, F32), (256, F32), (512, BF16), (512, F32), (256, F32), (512, BF16), (256, BF16)]
            gq, gk, gv, gr, ga, sq, skv = _in_projection(
                _even_in_kernel, "even_in", x, mod[l], nw_mix, table, [], weights, outs)
            oa = _gated_linear(gq, gk, ga, gv, gr, gla_norm_w[j].reshape(1, LANES), heads=2)
            ob = _sliding_window(swa_sinks[j], sq, skv)
            w_out = ev_w_out[j]
            wob = w_out[MIX_HALF:][swa_cols].astype(BF16)
        else:
            w = od_w_in[j]
            lb = lbs[j].reshape(1, -1)
            consts = [jnp.log(lb), jnp.log1p(-lb), 1.0 - lb]
            weights = [w.astype(BF16)]
            tiles = TM_PROJ // TQ_DIFF
            n_dh = MIX_HALF // LANES
            vt_out = ((bsz, seq // TQ_DIFF, n_dh, LANES, TQ_DIFF),
                      pl.BlockSpec((None, tiles, n_dh, LANES, TQ_DIFF), lambda b, t: (b, t, 0, 0, 0)), BF16)
            outs = [(512, BF16), (512, BF16), vt_out, (512, F32), (512, F32), (512, F32),
                    (512, BF16), (512, F32)]
            dq, dk, dvt, hq, hk, hf, hi, hg = _in_projection(
                _odd_in_kernel, "odd_in", x, mod[l], nw_mix, table, consts, weights, outs)
            lam_init = 0.8 - 0.6 * math.exp(-0.3 * l)
            lv = diff_lambda[j].astype(F32)
            lam = jnp.exp(jnp.sum(lv[0] * lv[1])) - jnp.exp(jnp.sum(lv[2] * lv[3])) + lam_init
            oa = _differential(lam.reshape(1), dq, dk, dvt, diff_norm_w[j].reshape(1, LANES), 1.0 - lam_init)
            ob = _gated_linear(hq, hk, hf, hi, hg, hgrn_norm_w[j].reshape(1, LANES), heads=1)
            w_out = od_w_out[j]
            wob = w_out[MIX_HALF:].astype(BF16)
        woa = w_out[:MIX_HALF].astype(BF16)
        x = _out_ffn(x, oa, ob, mod[l], norm_ffn_w[l].reshape(1, D_MODEL), fw, woa, wob,
                     ffn_wi, ffn_conv_w[l], ffn_conv_b[l].reshape(1, D_FF), ffn_wo,
                     layer=l, final=(l == DEPTH - 1))
    return x
```

```python
import functools
import math

import jax
import numpy as np
import jax.numpy as jnp
from jax import lax
from jax.experimental import pallas as pl
from jax.experimental.pallas import tpu as pltpu

F32 = jnp.float32
BF16 = jnp.bfloat16

D_MODEL = 1024
DEPTH = 4
HEAD_DIM = 64
ROT_DIMS = HEAD_DIM // 4
ROPE_THETA = 500000.0
GLA_GATE_RANK = 16
GLA_GATE_NORMALIZER = 16.0
N_SWA_HEADS = 8
N_SWA_KV_HEADS = 2
WINDOW = 128
HGRN_EXPAND = 128
D_FF = 2816
MIX_HALF = D_MODEL // 2
NORM_EPS = 1e-6
LOG2_E = math.log2(math.e)

LANES = 128
CHUNK = 64
SUB = 32
EXP_CLAMP = 60.0
TM_PROJ = 512
TM_IN = 1024
TM_FFN = 1024
GLA_QK = 512
GLA_V = 512
SWA_Q = 512
SWA_KV = 256
TS_GL = 2048
TQ_DIFF = 512
SWA_BLOCKS = 4
ONES_ROWS = 16
FF_CHUNK = 256
VMEM_LIMIT = 56 * 1024 * 1024

NT_DIMS = (((1,), (1,)), ((), ()))
TN_DIMS = (((0,), (0,)), ((), ()))


def _sigmoid(z):
    return 1.0 / (1.0 + jnp.exp(-z))


def _silu(z):
    return z * _sigmoid(z)


def _log_sigmoid(z):
    return jnp.minimum(z, 0.0) - jnp.log(1.0 + jnp.exp(-jnp.abs(z)))


def _dot(a, b):
    return jnp.dot(a, b, preferred_element_type=F32)


def _modulated_norm(x, nw, sc, sh):
    ms = jnp.mean(x * x, axis=-1, keepdims=True)
    return (x * lax.rsqrt(ms + NORM_EPS)) * (nw * (1.0 + sc)) + sh


def _head_norm(o, nw):
    ms = jnp.mean(o * o, axis=-1, keepdims=True)
    return (o * lax.rsqrt(ms + NORM_EPS)) * nw


def _rope_tables(tab):
    c, s = tab[:, :LANES], tab[:, LANES:]
    lane = lax.broadcasted_iota(jnp.int32, c.shape, 1)
    first_half = (lane & (HEAD_DIM - 1)) < ROT_DIMS // 2
    return c, jnp.where(first_half, s, 0.0), jnp.where(first_half, 0.0, s)


def _rope(x, tables):
    n = x.shape[-1] // LANES
    c, s_up, s_down = (jnp.concatenate([t] * n, axis=-1) if n > 1 else t for t in tables)
    half = ROT_DIMS // 2
    up = pltpu.roll(x, x.shape[-1] - half, 1)
    down = pltpu.roll(x, half, 1)
    return x * c + up * s_up + down * s_down


def _mod_kernel(c_ref, w_ref, b_ref, o_ref):
    ca = _silu(c_ref[...]).astype(BF16)
    o_ref[0] = _dot(ca, w_ref[0].astype(BF16)) + b_ref[0]


def _modulation(c, mod_w, mod_b):
    bsz = c.shape[0]
    tn = 1024
    n_out = mod_w.shape[-1]
    return pl.pallas_call(
        _mod_kernel,
        grid=(DEPTH, n_out // tn),
        in_specs=[
            pl.BlockSpec((bsz, D_MODEL), lambda l, j: (0, 0)),
            pl.BlockSpec((1, D_MODEL, tn), lambda l, j: (l, 0, j)),
            pl.BlockSpec((1, 1, tn), lambda l, j: (l, 0, j)),
        ],
        out_specs=pl.BlockSpec((1, bsz, tn), lambda l, j: (l, 0, j)),
        out_shape=jax.ShapeDtypeStruct((DEPTH, bsz, n_out), F32),
        name="modulation",
    )(c, mod_w, mod_b.reshape(DEPTH, 1, n_out))


def _even_in_kernel(x_ref, mod_ref, nw_ref, tab_ref,
                    wg_ref, wl_ref, gw_ref, gb_ref, wsq_ref, wskv_ref,
                    gq_ref, gk_ref, gv_ref, gr_ref, ga_ref, sq_ref, skv_ref):
    m = mod_ref[...]
    qk, qkv = GLA_QK, GLA_QK + GLA_V
    for r in range(x_ref.shape[0] // TM_PROJ):
        rows = slice(r * TM_PROJ, (r + 1) * TM_PROJ)
        h = _modulated_norm(x_ref[rows, :], nw_ref[...], m[:, D_MODEL:2 * D_MODEL], m[:, 0:D_MODEL]).astype(BF16)
        glr = _dot(h, wl_ref[...]).astype(BF16)
        gq = _dot(h, wg_ref[:, 0:qk // 2])
        gk = _dot(h, wg_ref[:, qk // 2:qk])
        gv = _dot(h, wg_ref[:, qk:qkv])
        gr = _dot(h, wg_ref[:, qkv:qkv + GLA_V])
        sq = _dot(h, wsq_ref[...])
        skv = _dot(h, wskv_ref[...])
        gate = _dot(glr, gw_ref[...]) + gb_ref[...]
        gq_ref[rows, :] = gq * (HEAD_DIM ** -0.5)
        gk_ref[rows, :] = gk
        gv_ref[rows, :] = gv.astype(BF16)
        gr_ref[rows, :] = _silu(gr)
        ga_ref[rows, :] = _log_sigmoid(gate) / GLA_GATE_NORMALIZER
        tab = _rope_tables(tab_ref[rows, :])
        sq_ref[rows, :] = (_rope(sq, tab) * (LOG2_E * HEAD_DIM ** -0.5)).astype(BF16)
        sk = _rope(skv[:, :LANES], tab)
        skv_ref[rows, :] = jnp.concatenate([sk, skv[:, LANES:]], axis=-1).astype(BF16)


def _odd_in_kernel(x_ref, mod_ref, nw_ref, tab_ref, llb_ref, l1m_ref, oml_ref, w_ref,
                   dq_ref, dk_ref, dvt_ref, hq_ref, hk_ref, hf_ref, hi_ref, hg_ref):
    wdq_ref, wdk_ref, wdv_ref, whq_ref, whf_ref, whi_ref, whg_ref = (
        w_ref.at[:, i * MIX_HALF:(i + 1) * MIX_HALF] for i in range(7))
    m = mod_ref[...]
    tiles = TM_PROJ // TQ_DIFF
    for r in range(x_ref.shape[0] // TM_PROJ):
        rows = slice(r * TM_PROJ, (r + 1) * TM_PROJ)
        h = _modulated_norm(x_ref[rows, :], nw_ref[...], m[:, D_MODEL:2 * D_MODEL], m[:, 0:D_MODEL]).astype(BF16)
        tab = _rope_tables(tab_ref[rows, :])
        z, hq, hg, dq, dk, dv, hi = (_dot(h, w[...]) for w in
                                     (whf_ref, whq_ref, whg_ref, wdq_ref, wdk_ref, wdv_ref, whi_ref))
        ez = jnp.exp(-jnp.abs(z))
        a1 = llb_ref[...]
        a2 = l1m_ref[...] + (jnp.minimum(z, 0.0) - jnp.log(1.0 + ez))
        hf_ref[rows, :] = jnp.maximum(a1, a2) + jnp.log(1.0 + jnp.exp(-jnp.abs(a1 - a2)))
        hk_ref[rows, :] = oml_ref[...] * (jnp.where(z >= 0.0, ez, 1.0) / (1.0 + ez))
        hq_ref[rows, :] = _silu(hq) * (HGRN_EXPAND ** -0.5)
        hg_ref[rows, :] = _silu(hg)
        dq_ref[rows, :] = (_rope(dq, tab) * (LOG2_E * HEAD_DIM ** -0.5)).astype(BF16)
        dk_ref[rows, :] = _rope(dk, tab).astype(BF16)
        for t in range(tiles):
            for hd in range(dvt_ref.shape[1]):
                tile = dv[t * TQ_DIFF:(t + 1) * TQ_DIFF, hd * LANES:(hd + 1) * LANES]
                dvt_ref[r * tiles + t, hd] = tile.T.astype(BF16)
        hi_ref[rows, :] = hi.astype(BF16)


def _tok_spec(width):
    return pl.BlockSpec((None, TM_IN, width), lambda b, t: (b, t, 0))


def _const_spec(shape):
    nd = len(shape)
    return pl.BlockSpec(shape, lambda b, t: (0,) * nd)


def _in_projection(kern, name, x, mod_l, nw, table, consts, weights, outs):
    bsz, seq, _ = x.shape
    in_specs = ([_tok_spec(D_MODEL), pl.BlockSpec((None, 1, 6 * D_MODEL), lambda b, t: (b, 0, 0)),
                 _const_spec(nw.shape), _tok_spec(2 * LANES)]
                + [_const_spec(a.shape) for a in consts]
                + [_const_spec(w.shape) for w in weights])
    out_specs = [_tok_spec(o[0]) if len(o) == 2 else o[1] for o in outs]
    out_shape = [jax.ShapeDtypeStruct((bsz, seq, o[0]) if len(o) == 2 else o[0], o[-1]) for o in outs]
    return pl.pallas_call(
        kern,
        grid=(bsz, seq // TM_IN),
        in_specs=in_specs,
        out_specs=out_specs,
        out_shape=out_shape,
        compiler_params=pltpu.CompilerParams(
            dimension_semantics=("arbitrary", "arbitrary"), vmem_limit_bytes=VMEM_LIMIT),
        name=name,
    )(x, mod_l, nw, table, *consts, *weights)


def _gl_kernel(q_ref, k_ref, g_ref, v_ref, gate_ref, nw_ref, o_ref, st_ref, cum_ref, inter_ref, *, heads):
    @pl.when(pl.program_id(2) == 0)
    def _():
        st_ref[...] = jnp.zeros_like(st_ref)

    dk = LANES // heads
    row = lax.broadcasted_iota(jnp.int32, (CHUNK, CHUNK), 0)
    col = lax.broadcasted_iota(jnp.int32, (CHUNK, CHUNK), 1)
    causal = row >= col
    tri = jnp.where(causal, 1.0, 0.0).astype(BF16)
    lane_sub = lax.broadcasted_iota(jnp.int32, (SUB, LANES), 1)
    lane_chunk = lax.broadcasted_iota(jnp.int32, (CHUNK, LANES), 1)
    nw = nw_ref[...]
    n_sub = CHUNK // SUB
    zeros_sub = jnp.zeros((SUB, LANES), BF16)

    def own(t, lane, p):
        return t if heads == 1 else jnp.where((lane >= p * dk) & (lane < (p + 1) * dk), t, 0.0)

    chunks = [slice(c * CHUNK, (c + 1) * CHUNK) for c in range(q_ref.shape[0] // CHUNK)]
    heads_r = range(heads)

    def sub_block_decay(g):
        return functools.reduce(jnp.maximum, [-jnp.sum(g[SUB * a:SUB * (a + 1)], axis=0, keepdims=True)
                                              for a in range(n_sub)])

    cum = []
    spread = jnp.zeros((1, LANES), F32)
    for rows in chunks:
        g = g_ref[rows, :]
        spread = jnp.maximum(spread, sub_block_decay(g))
        g1 = g.astype(BF16)
        r1 = g - g1.astype(F32)
        g2 = r1.astype(BF16)
        g3 = (r1 - g2.astype(F32)).astype(BF16)
        cum.append(_dot(tri, g1) + _dot(tri, g2) + _dot(tri, g3))

    q_in, decay_tot, k_own, k_stack, q_stack = [], [], [], [], []
    for rows, b in zip(chunks, cum):
        q = q_ref[rows, :]
        k = k_ref[rows, :]
        btot = b[CHUNK - 1:CHUNK, :]
        q_in.append((q * jnp.exp(b)).astype(BF16))
        k_out = k * jnp.exp(btot - b)
        k_own.append([own(k_out, lane_chunk, p).astype(BF16) for p in heads_r])
        decay_tot.append(jnp.exp(btot))
        k_groups, q_groups = [], []
        for a in range(n_sub):
            hi = SUB * (a + 1)
            lvl = b[SUB * a - 1:SUB * a, :] if a else jnp.zeros((1, LANES), F32)
            kk = (k[:hi] * jnp.exp(jnp.minimum(lvl - b[:hi], EXP_CLAMP))).astype(BF16)
            if hi < CHUNK:
                kk = jnp.concatenate([kk, jnp.zeros((CHUNK - hi, LANES), BF16)], axis=0)
            k_groups.append(kk)
            q_groups.append(q[SUB * a:hi] * jnp.exp(b[SUB * a:hi] - lvl))
        k_stack.append(jnp.concatenate(k_groups, axis=1))
        per_head = []
        for p in heads_r:
            q_rows = []
            for a in range(n_sub):
                qa = own(q_groups[a], lane_sub, p).astype(BF16)
                q_rows.append(jnp.concatenate([zeros_sub] * a + [qa] + [zeros_sub] * (n_sub - 1 - a), axis=1))
            per_head.append(jnp.concatenate(q_rows, axis=0))
        q_stack.append(per_head)

    vals = [[v_ref[rows, p * LANES:(p + 1) * LANES] for p in heads_r] for rows in chunks]
    scores = [[lax.dot_general(q_stack[c][p], k_stack[c], NT_DIMS, preferred_element_type=F32)
               for p in heads_r] for c in range(len(chunks))]
    incr = [[lax.dot_general(vals[c][p], k_own[c][p], TN_DIMS, preferred_element_type=F32)
             for p in heads_r] for c in range(len(chunks))]

    outs = []
    for p in heads_r:
        st = st_ref[p]
        for c in range(len(chunks)):
            s = jnp.where(causal, scores[c][p], 0.0).astype(BF16)
            inter = lax.dot_general(q_in[c], st.astype(BF16), NT_DIMS, preferred_element_type=F32)
            inter_ref[p, chunks[c], :] = inter
            o = _dot(s, vals[c][p]) + inter
            outs.append((c, p, o))
            st = st * decay_tot[c] + incr[c][p]
        st_ref[p] = st

    def finish(rows, p, o):
        cols = slice(p * LANES, (p + 1) * LANES)
        o_ref[rows, cols] = (_head_norm(o, nw) * gate_ref[rows, cols]).astype(BF16)

    for c, p, o in outs:
        finish(chunks[c], p, o)

    for c, b in enumerate(cum):
        cum_ref[chunks[c], :] = b

    @pl.when(jnp.max(spread) > EXP_CLAMP)
    def _():
        def redo_chunk(c, carry):
            r0 = pl.multiple_of(c * CHUNK, CHUNK)
            rows = pl.ds(r0, CHUNK)

            @pl.when(jnp.max(sub_block_decay(g_ref[rows, :])) > EXP_CLAMP)
            def _():
                q = q_ref[rows, :]
                b = cum_ref[rows, :]

                def key_column(j, acc):
                    kj = k_ref[pl.ds(r0 + j, 1), :]
                    bj = cum_ref[pl.ds(r0 + j, 1), :]
                    t = q * kj * jnp.exp(jnp.minimum(b - bj, 0.0))
                    return tuple(jnp.where(col == j, jnp.sum(own(t, lane_chunk, p), axis=1, keepdims=True),
                                           acc[p]) for p in heads_r)

                exact = lax.fori_loop(0, CHUNK, key_column,
                                      tuple(jnp.zeros((CHUNK, CHUNK), F32) for _ in heads_r))
                for p in heads_r:
                    s = jnp.where(causal, exact[p], 0.0).astype(BF16)
                    finish(rows, p, _dot(s, v_ref[rows, p * LANES:(p + 1) * LANES]) + inter_ref[p, rows, :])
            return carry

        lax.fori_loop(0, len(chunks), redo_chunk, 0)


def _gated_linear(q, k, g, v, gate, nw, heads):
    bsz, seq, width = q.shape
    groups = width // LANES
    vw = heads * LANES
    qspec = pl.BlockSpec((None, TS_GL, LANES), lambda b, gi, s: (b, s, gi))
    vspec = pl.BlockSpec((None, TS_GL, vw), lambda b, gi, s: (b, s, gi))
    return pl.pallas_call(
        functools.partial(_gl_kernel, heads=heads),
        grid=(bsz, groups, seq // TS_GL),
        in_specs=[qspec, qspec, qspec, vspec, vspec, pl.BlockSpec((1, LANES), lambda b, gi, s: (0, 0))],
        out_specs=vspec,
        out_shape=jax.ShapeDtypeStruct((bsz, seq, groups * vw), BF16),
        scratch_shapes=[pltpu.VMEM((heads, LANES, LANES), F32), pltpu.VMEM((TS_GL, LANES), F32),
                        pltpu.VMEM((heads, TS_GL, LANES), F32)],
        compiler_params=pltpu.CompilerParams(
            dimension_semantics=("arbitrary", "arbitrary", "arbitrary"), vmem_limit_bytes=VMEM_LIMIT),
        name=f"gated_linear_{heads}",
    )(q, k, g, v, gate, nw)


def _swa_kernel(sink_ref, q_ref, cur_ref, prev_ref, o_ref):
    n = pl.program_id(1)
    kj = lax.broadcasted_iota(jnp.int32, (2 * WINDOW, WINDOW), 0)
    qi = lax.broadcasted_iota(jnp.int32, (2 * WINDOW, WINDOW), 1) + WINDOW
    rel = qi - kj
    band = (rel >= 0) & (rel < WINDOW)
    band_first = band & ((n > 0) | (kj >= WINDOW))
    lane = lax.broadcasted_iota(jnp.int32, (WINDOW, LANES), 1)
    low = lane < HEAD_DIM
    per_kv = N_SWA_HEADS // N_SWA_KV_HEADS
    chains = [(t, i, kv) for t in range(q_ref.shape[0] // WINDOW)
              for i in range(per_kv) for kv in range(N_SWA_KV_HEADS)]
    values, scores = {}, {}
    for t in range(q_ref.shape[0] // WINDOW):
        rows = slice(t * WINDOW, (t + 1) * WINDOW)
        cur = cur_ref[rows, :]
        prev = cur_ref[(t - 1) * WINDOW:t * WINDOW, :] if t else prev_ref[...]
        keys = jnp.concatenate([prev[:, :LANES], cur[:, :LANES]], axis=0)
        vals = jnp.concatenate([prev[:, LANES:], cur[:, LANES:]], axis=0)
        vlow = lax.broadcasted_iota(jnp.int32, vals.shape, 1) < HEAD_DIM
        one = jnp.ones_like(vals)
        values[t] = (jnp.where(vlow, vals, one), jnp.where(vlow, one, vals))
        for i in range(per_kv):
            qg = q_ref[rows, i * LANES:(i + 1) * LANES]
            zero = jnp.zeros_like(qg)
            both = jnp.concatenate([jnp.where(low, qg, zero), jnp.where(low, zero, qg)], axis=0)
            s2 = lax.dot_general(keys, both, NT_DIMS, preferred_element_type=F32)
            scores[t, i, 0], scores[t, i, 1] = s2[:, :WINDOW], s2[:, WINDOW:]
    probs, sink_terms = {}, {}
    for t, i, kv in chains:
        sink = sink_ref[kv * per_kv + i] * LOG2_E
        s = jnp.where(band if t else band_first, scores[t, i, kv], -jnp.inf)
        m = jnp.maximum(jnp.max(s, axis=0, keepdims=True), sink)
        probs[t, i, kv] = jnp.exp2(s - m).astype(BF16)
        sink_terms[t, i, kv] = jnp.exp2(sink - m)
    outs = {}
    for t, i, kv in chains:
        if i % 2:
            continue
        pair = jnp.concatenate([probs[t, i, kv], probs[t, i + 1, kv]], axis=1)
        pv2 = lax.dot_general(values[t][kv], pair, TN_DIMS, preferred_element_type=F32)
        denom_row = HEAD_DIM if kv == 0 else 0
        for off, ii in ((0, i), (WINDOW, i + 1)):
            pv = pv2[:, off:off + WINDOW]
            outs[t, ii, kv] = pv * (1.0 / (pv[denom_row:denom_row + 1, :] + sink_terms[t, ii, kv]))
    top = lax.broadcasted_iota(jnp.int32, (LANES, WINDOW), 0) < HEAD_DIM
    for t, i, kv in chains:
        if kv == 0:
            o_t = jnp.where(top, outs[t, i, 0], outs[t, i, 1])
            o_ref[t * WINDOW:(t + 1) * WINDOW, i * LANES:(i + 1) * LANES] = o_t.T.astype(BF16)


def _sliding_window(sinks, q, kv):
    bsz, seq, _ = q.shape
    tq = SWA_BLOCKS * WINDOW
    return pl.pallas_call(
        _swa_kernel,
        grid=(bsz, seq // tq),
        in_specs=[
            pl.BlockSpec(memory_space=pltpu.SMEM),
            pl.BlockSpec((None, tq, MIX_HALF), lambda b, n: (b, n, 0)),
            pl.BlockSpec((None, tq, 2 * LANES), lambda b, n: (b, n, 0)),
            pl.BlockSpec((None, WINDOW, 2 * LANES), lambda b, n: (b, jnp.maximum(SWA_BLOCKS * n - 1, 0), 0)),
        ],
        out_specs=pl.BlockSpec((None, tq, MIX_HALF), lambda b, n: (b, n, 0)),
        out_shape=jax.ShapeDtypeStruct((bsz, seq, MIX_HALF), BF16),
        compiler_params=pltpu.CompilerParams(
            dimension_semantics=("arbitrary", "arbitrary"), vmem_limit_bytes=VMEM_LIMIT),
        name="sliding_window",
    )(sinks, q, kv, kv)


def _diff_kernel(lam_ref, q_ref, k_ref, vt_ref, nw_ref, o_ref, acc_ref, max_ref, s_ref, *, out_scale):
    qi = pl.program_id(1)
    tq = q_ref.shape[0]
    nh = q_ref.shape[1] // LANES
    lane = lax.broadcasted_iota(jnp.int32, (tq, LANES), 1)
    qs = []
    for h in range(nh):
        q = q_ref[:, h * LANES:(h + 1) * LANES]
        zero = jnp.zeros_like(q)
        qs += [jnp.where(lane < HEAD_DIM, q, zero), jnp.where(lane >= HEAD_DIM, q, zero)]
    key = lax.broadcasted_iota(jnp.int32, (tq, tq), 0)
    qry = lax.broadcasted_iota(jnp.int32, (tq, tq), 1)
    causal = key <= qry
    ones = jnp.ones((acc_ref.shape[1] - LANES, tq), BF16)
    acc_ref[...] = jnp.zeros_like(acc_ref)
    max_ref[...] = jnp.full(max_ref.shape, -jnp.inf, F32)

    def issue_scores(j, c):
        rows = pl.ds(pl.multiple_of(j * tq, tq), tq)
        kt = k_ref[rows, (c // 2) * LANES:(c // 2 + 1) * LANES]
        s_ref[c] = lax.dot_general(kt, qs[c], NT_DIMS, preferred_element_type=F32)

    def consume(j, c, masked):
        s = s_ref[c]
        if masked:
            s = jnp.where(causal, s, -jnp.inf)
        m = max_ref[c:c + 1, :]
        m_new = jnp.maximum(m, jnp.max(s, axis=0, keepdims=True))
        max_ref[c:c + 1, :] = m_new
        p = jnp.exp2(s - m_new).astype(BF16)
        vt1 = jnp.concatenate([vt_ref[j, c // 2], ones], axis=0)
        acc_ref[c] = jnp.exp2(m - m_new) * acc_ref[c] + _dot(vt1, p)

    for c in range(2 * nh):
        issue_scores(0, c)

    def body(j, carry):
        for c in range(2 * nh):
            consume(j, c, False)
            issue_scores(j + 1, c)
        return carry

    lax.fori_loop(0, qi, body, 0)
    for c in range(2 * nh):
        consume(qi, c, True)
    lam = lam_ref[0]
    nw = nw_ref[...]
    for h in range(nh):
        a1, a2 = acc_ref[2 * h], acc_ref[2 * h + 1]
        o_t = a1[:LANES] / a1[LANES:LANES + 1] - lam * (a2[:LANES] / a2[LANES:LANES + 1])
        o_ref[:, h * LANES:(h + 1) * LANES] = (_head_norm(o_t.T, nw) * out_scale).astype(BF16)


def _differential(lam, q, k, vt, nw, out_scale):
    bsz, seq, width = q.shape
    nh = width // LANES
    nt = seq // TQ_DIFF
    q_spec = pl.BlockSpec((None, TQ_DIFF, width), lambda b, i: (b, i, 0))
    return pl.pallas_call(
        functools.partial(_diff_kernel, out_scale=out_scale),
        grid=(bsz, nt),
        in_specs=[pl.BlockSpec(memory_space=pltpu.SMEM), q_spec,
                  pl.BlockSpec((None, seq, width), lambda b, i: (b, 0, 0)),
                  pl.BlockSpec((None, nt, nh, LANES, TQ_DIFF), lambda b, i: (b, 0, 0, 0, 0)),
                  pl.BlockSpec((1, LANES), lambda b, i: (0, 0))],
        out_specs=q_spec,
        out_shape=jax.ShapeDtypeStruct((bsz, seq, width), BF16),
        scratch_shapes=[pltpu.VMEM((2 * nh, LANES + ONES_ROWS, TQ_DIFF), F32),
                        pltpu.VMEM((2 * nh, TQ_DIFF), F32),
                        pltpu.VMEM((2 * nh, TQ_DIFF, TQ_DIFF), F32)],
        compiler_params=pltpu.CompilerParams(
            dimension_semantics=("arbitrary", "arbitrary"), vmem_limit_bytes=VMEM_LIMIT),
        name="differential",
    )(lam, q, k, vt, nw)


def _out_ffn_kernel(x_ref, oa_ref, ob_ref, mod_ref, nw_ref, fw_ref, woa_ref, wob_ref,
                    wi_ref, cw_ref, cb_ref, wo_ref,
                    y_ref, act_ref, carry_ref, *, final):
    t = pl.program_id(1)
    m = mod_ref[...]
    d = D_MODEL
    g1, sh2, sc2, g2 = m[:, 2 * d:3 * d], m[:, 3 * d:4 * d], m[:, 4 * d:5 * d], m[:, 5 * d:6 * d]
    @pl.when(t == 0)
    def _():
        carry_ref[...] = jnp.zeros_like(carry_ref)

    tm = act_ref.shape[0]
    row = lax.broadcasted_iota(jnp.int32, (tm, FF_CHUNK), 0)
    for r in range(x_ref.shape[0] // tm):
        rows = slice(r * tm, (r + 1) * tm)
        mix = _dot(oa_ref[rows, :], woa_ref[...]) + _dot(ob_ref[rows, :], wob_ref[...])
        x1 = x_ref[rows, :] + g1 * mix
        h = _modulated_norm(x1, nw_ref[...], sc2, sh2).astype(BF16)
        for c in range(D_FF // FF_CHUNK):
            cols = slice(c * FF_CHUNK, (c + 1) * FF_CHUNK)
            a = _dot(h, wi_ref[:, cols])
            u = _dot(h, wi_ref[:, D_FF + c * FF_CHUNK:D_FF + (c + 1) * FF_CHUNK])
            tail = carry_ref[:, cols]
            cw = cw_ref[:, cols]
            prev1 = jnp.where(row < 1, tail[7:8, :], pltpu.roll(a, 1, 0))
            prev2 = jnp.where(row < 1, tail[6:7, :], jnp.where(row < 2, tail[7:8, :], pltpu.roll(a, 2, 0)))
            carry_ref[:, cols] = a[tm - 8:tm, :]
            conv = prev2 * cw[0:1, :] + prev1 * cw[1:2, :] + a * cw[2:3, :] + cb_ref[:, cols]
            act_ref[:, cols] = (_silu(conv) * u).astype(BF16)
        y = x1 + g2 * _dot(act_ref[...], wo_ref[...])
        if final:
            ms = jnp.mean(y * y, axis=-1, keepdims=True)
            y = (y * lax.rsqrt(ms + NORM_EPS)) * fw_ref[...]
        y_ref[rows, :] = y


def _out_ffn(x, oa, ob, mod_l, nw, fw, woa, wob, wi_all, cw, cb, wo_all, layer, final):
    bsz, seq, _ = x.shape

    def resident(shape):
        nd = len(shape)
        return pl.BlockSpec(shape, lambda b, t: (0,) * nd, pipeline_mode=pl.Buffered(1))

    def layer_resident(shape):
        return pl.BlockSpec((None,) + shape[1:], lambda b, t: (layer, 0, 0), pipeline_mode=pl.Buffered(1))

    def tok(width):
        return pl.BlockSpec((None, TM_FFN, width), lambda b, t: (b, t, 0))

    return pl.pallas_call(
        functools.partial(_out_ffn_kernel, final=final),
        grid=(bsz, seq // TM_FFN),
        in_specs=[tok(D_MODEL), tok(MIX_HALF), tok(MIX_HALF),
                  pl.BlockSpec((None, 1, 6 * D_MODEL), lambda b, t: (b, 0, 0)),
                  _const_spec(nw.shape), _const_spec(fw.shape),
                  resident(woa.shape), resident(wob.shape), layer_resident(wi_all.shape),
                  _const_spec(cw.shape), _const_spec(cb.shape), layer_resident(wo_all.shape)],
        out_specs=tok(D_MODEL),
        out_shape=jax.ShapeDtypeStruct(x.shape, F32),
        scratch_shapes=[pltpu.VMEM((TM_PROJ, D_FF), BF16), pltpu.VMEM((8, D_FF), F32)],
        compiler_params=pltpu.CompilerParams(
            dimension_semantics=("arbitrary", "arbitrary"), vmem_limit_bytes=VMEM_LIMIT),
        name="out_ffn",
    )(x, oa, ob, mod_l, nw, fw, woa, wob, wi_all, cw, cb, wo_all)


def _rope_lane_table(positions):
    half = ROT_DIMS // 2
    inv_freq = ROPE_THETA ** (-jnp.arange(0, ROT_DIMS, 2, dtype=F32) / ROT_DIMS)
    ang = inv_freq[:, None] * positions.astype(F32).reshape(1, -1)
    cs = jnp.concatenate([jnp.cos(ang), jnp.sin(ang)], axis=0)
    d = np.arange(LANES) % HEAD_DIM
    sel = np.zeros((2 * half, 2 * LANES), np.float32)
    base = np.zeros((1, 2 * LANES), np.float32)
    for lane in range(LANES):
        if d[lane] < ROT_DIMS:
            sel[d[lane] % half, lane] = 1.0
            sel[half + d[lane] % half, LANES + lane] = -1.0 if d[lane] < half else 1.0
        else:
            base[0, lane] = 1.0
    tab = lax.dot_general(cs, jnp.asarray(sel), TN_DIMS, precision=lax.Precision.HIGHEST) + base
    return tab.reshape(positions.shape + (2 * LANES,))


def _cols(w, start, width):
    return w[:, start:start + width].astype(BF16)


SWA_HEAD_ORDER = (0, 4, 1, 5, 2, 6, 3, 7)


def kernel(x, c, positions, mod_w, mod_b, norm_mix_w, norm_ffn_w, ev_w_in, gla_gate_w, gla_gate_b,
           gla_norm_w, swa_sinks, ev_w_out, od_w_in, diff_lambda, diff_norm_w, hgrn_lb_logits,
           hgrn_norm_w, od_w_out, ffn_w_in, ffn_conv_w, ffn_conv_b, ffn_w_out, final_norm_w):
    bsz, seq, _ = x.shape
    table = _rope_lane_table(positions)
    lbs = jnp.cumsum(jax.nn.softmax(hgrn_lb_logits.astype(F32), axis=0), axis=0)
    lbs = lbs - lbs[0:1]
    mod = _modulation(c, mod_w, mod_b).reshape(DEPTH, bsz, 1, 6 * D_MODEL)
    swa_cols = jnp.concatenate([jnp.arange(HEAD_DIM) + HEAD_DIM * h for h in SWA_HEAD_ORDER])
    fw = final_norm_w.reshape(1, D_MODEL)
    ffn_wi, ffn_wo = ffn_w_in.astype(BF16), ffn_w_out.astype(BF16)

    for l in range(DEPTH):
        j = l // 2
        nw_mix = norm_mix_w[l].reshape(1, D_MODEL)
        if l % 2 == 0:
            w = ev_w_in[j]
            n_gla = GLA_QK + 2 * GLA_V
            sq_start = n_gla + GLA_GATE_RANK
            w_sq = w[:, sq_start:sq_start + SWA_Q][:, swa_cols].astype(BF16)
            w_glr = jnp.pad(w[:, n_gla:sq_start], ((0, 0), (0, LANES - GLA_GATE_RANK))).astype(BF16)
            gate_w = jnp.pad(gla_gate_w[j], ((0, LANES - GLA_GATE_RANK), (0, 0))).astype(BF16)
            weights = [_cols(w, 0, n_gla), w_glr, gate_w, gla_gate_b[j].reshape(1, -1), w_sq,
                       _cols(w, sq_start + SWA_Q, SWA_KV)]
            outs = [(GLA_QK // 2, F32), (GLA_QK // 2, F32), (GLA_V, BF16), (GLA_V, F32), (GLA_QK // 2, F32),
                    (SWA_Q, BF16), (SWA_KV, BF16)]
            gq, gk, gv, gr, ga, sq, skv = _in_projection(
                _even_in_kernel, "even_in", x, mod[l], nw_mix, table, [], weights, outs)
            oa = _gated_linear(gq, gk, ga, gv, gr, gla_norm_w[j].reshape(1, LANES), heads=2)
            ob = _sliding_window(swa_sinks[j], sq, skv)
            w_out = ev_w_out[j]
            wob = w_out[MIX_HALF:][swa_cols].astype(BF16)
        else:
            w = od_w_in[j]
            lb = lbs[j].reshape(1, -1)
            consts = [jnp.log(lb), jnp.log1p(-lb), 1.0 - lb]
            weights = [w.astype(BF16)]
            tiles = TM_IN // TQ_DIFF
            n_dh = MIX_HALF // LANES
            vt_out = ((bsz, seq // TQ_DIFF, n_dh, LANES, TQ_DIFF),
                      pl.BlockSpec((None, tiles, n_dh, LANES, TQ_DIFF), lambda b, t: (b, t, 0, 0, 0)), BF16)
            outs = [(512, BF16), (512, BF16), vt_out, (512, F32), (512, F32), (512, F32),
                    (512, BF16), (512, F32)]
            dq, dk, dvt, hq, hk, hf, hi, hg = _in_projection(
                _odd_in_kernel, "odd_in", x, mod[l], nw_mix, table, consts, weights, outs)
            lam_init = 0.8 - 0.6 * math.exp(-0.3 * l)
            lv = diff_lambda[j].astype(F32)
            lam = jnp.exp(jnp.sum(lv[0] * lv[1])) - jnp.exp(jnp.sum(lv[2] * lv[3])) + lam_init
            oa = _differential(lam.reshape(1), dq, dk, dvt, diff_norm_w[j].reshape(1, LANES), 1.0 - lam_init)
            ob = _gated_linear(hq, hk, hf, hi, hg, hgrn_norm_w[j].reshape(1, LANES), heads=1)
            w_out = od_w_out[j]
            wob = w_out[MIX_HALF:].astype(BF16)
        woa = w_out[:MIX_HALF].astype(BF16)
        x = _out_ffn(x, oa, ob, mod[l], norm_ffn_w[l].reshape(1, D_MODEL), fw, woa, wob,
                     ffn_wi, ffn_conv_w[l], ffn_conv_b[l].reshape(1, D_FF), ffn_wo,
                     layer=l, final=(l == DEPTH - 1))
    return x
```

```python
import functools
import math

import jax
import numpy as np
import jax.numpy as jnp
from jax import lax
from jax.experimental import pallas as pl
from jax.experimental.pallas import tpu as pltpu

F32 = jnp.float32
BF16 = jnp.bfloat16

D_MODEL = 1024
DEPTH = 4
HEAD_DIM = 64
ROT_DIMS = HEAD_DIM // 4
ROPE_THETA = 500000.0
GLA_GATE_RANK = 16
GLA_GATE_NORMALIZER = 16.0
N_SWA_HEADS = 8
N_SWA_KV_HEADS = 2
WINDOW = 128
HGRN_EXPAND = 128
D_FF = 2816
MIX_HALF = D_MODEL // 2
NORM_EPS = 1e-6
LOG2_E = math.log2(math.e)

LANES = 128
CHUNK = 64
SUB = 32
EXP_CLAMP = 60.0
TM_PROJ = 512
TM_IN = 1024
TM_FFN = 1024
GLA_QK = 512
GLA_V = 512
SWA_Q = 512
SWA_KV = 256
TS_GL = 2048
TQ_DIFF = 512
SWA_BLOCKS = 8
ONES_ROWS = 16
FF_CHUNK = 256
VMEM_LIMIT = 56 * 1024 * 1024

NT_DIMS = (((1,), (1,)), ((), ()))
TN_DIMS = (((0,), (0,)), ((), ()))


def _sigmoid(z):
    return 1.0 / (1.0 + jnp.exp(-z))


def _silu(z):
    return z * _sigmoid(z)


def _log_sigmoid(z):
    return jnp.minimum(z, 0.0) - jnp.log(1.0 + jnp.exp(-jnp.abs(z)))


def _dot(a, b):
    return jnp.dot(a, b, preferred_element_type=F32)


def _modulated_norm(x, nw, sc, sh):
    ms = jnp.mean(x * x, axis=-1, keepdims=True)
    return (x * lax.rsqrt(ms + NORM_EPS)) * (nw * (1.0 + sc)) + sh


def _head_norm(o, nw):
    ms = jnp.mean(o * o, axis=-1, keepdims=True)
    return (o * lax.rsqrt(ms + NORM_EPS)) * nw


def _rope_tables(tab):
    c, s = tab[:, :LANES], tab[:, LANES:]
    lane = lax.broadcasted_iota(jnp.int32, c.shape, 1)
    first_half = (lane & (HEAD_DIM - 1)) < ROT_DIMS // 2
    return c, jnp.where(first_half, s, 0.0), jnp.where(first_half, 0.0, s)


def _rope(x, tables):
    n = x.shape[-1] // LANES
    c, s_up, s_down = (jnp.concatenate([t] * n, axis=-1) if n > 1 else t for t in tables)
    half = ROT_DIMS // 2
    up = pltpu.roll(x, x.shape[-1] - half, 1)
    down = pltpu.roll(x, half, 1)
    return x * c + up * s_up + down * s_down


def _mod_kernel(c_ref, w_ref, b_ref, o_ref):
    ca = _silu(c_ref[...]).astype(BF16)
    o_ref[0] = _dot(ca, w_ref[0].astype(BF16)) + b_ref[0]


def _modulation(c, mod_w, mod_b):
    bsz = c.shape[0]
    tn = 1024
    n_out = mod_w.shape[-1]
    return pl.pallas_call(
        _mod_kernel,
        grid=(DEPTH, n_out // tn),
        in_specs=[
            pl.BlockSpec((bsz, D_MODEL), lambda l, j: (0, 0)),
            pl.BlockSpec((1, D_MODEL, tn), lambda l, j: (l, 0, j)),
            pl.BlockSpec((1, 1, tn), lambda l, j: (l, 0, j)),
        ],
        out_specs=pl.BlockSpec((1, bsz, tn), lambda l, j: (l, 0, j)),
        out_shape=jax.ShapeDtypeStruct((DEPTH, bsz, n_out), F32),
        name="modulation",
    )(c, mod_w, mod_b.reshape(DEPTH, 1, n_out))


def _even_in_kernel(x_ref, mod_ref, nw_ref, tab_ref,
                    wg_ref, wl_ref, gw_ref, gb_ref, wsq_ref, wskv_ref,
                    gq_ref, gk_ref, gv_ref, gr_ref, ga_ref, sq_ref, skv_ref):
    m = mod_ref[...]
    qk, qkv = GLA_QK, GLA_QK + GLA_V
    for r in range(x_ref.shape[0] // TM_PROJ):
        rows = slice(r * TM_PROJ, (r + 1) * TM_PROJ)
        h = _modulated_norm(x_ref[rows, :], nw_ref[...], m[:, D_MODEL:2 * D_MODEL], m[:, 0:D_MODEL]).astype(BF16)
        glr = _dot(h, wl_ref[...]).astype(BF16)
        gq = _dot(h, wg_ref[:, 0:qk // 2])
        gk = _dot(h, wg_ref[:, qk // 2:qk])
        gv = _dot(h, wg_ref[:, qk:qkv])
        gr = _dot(h, wg_ref[:, qkv:qkv + GLA_V])
        sq = _dot(h, wsq_ref[...])
        skv = _dot(h, wskv_ref[...])
        gate = _dot(glr, gw_ref[...]) + gb_ref[...]
        gq_ref[rows, :] = gq * (HEAD_DIM ** -0.5)
        gk_ref[rows, :] = gk
        gv_ref[rows, :] = gv.astype(BF16)
        gr_ref[rows, :] = _silu(gr)
        ga_ref[rows, :] = _log_sigmoid(gate) / GLA_GATE_NORMALIZER
        tab = _rope_tables(tab_ref[rows, :])
        sq_ref[rows, :] = (_rope(sq, tab) * (LOG2_E * HEAD_DIM ** -0.5)).astype(BF16)
        sk = _rope(skv[:, :LANES], tab)
        skv_ref[rows, :] = jnp.concatenate([sk, skv[:, LANES:]], axis=-1).astype(BF16)


def _odd_in_kernel(x_ref, mod_ref, nw_ref, tab_ref, llb_ref, l1m_ref, oml_ref, w_ref,
                   dq_ref, dk_ref, dvt_ref, hq_ref, hk_ref, hf_ref, hi_ref, hg_ref):
    wdq_ref, wdk_ref, wdv_ref, whq_ref, whf_ref, whi_ref, whg_ref = (
        w_ref.at[:, i * MIX_HALF:(i + 1) * MIX_HALF] for i in range(7))
    m = mod_ref[...]
    tiles = TM_PROJ // TQ_DIFF
    for r in range(x_ref.shape[0] // TM_PROJ):
        rows = slice(r * TM_PROJ, (r + 1) * TM_PROJ)
        h = _modulated_norm(x_ref[rows, :], nw_ref[...], m[:, D_MODEL:2 * D_MODEL], m[:, 0:D_MODEL]).astype(BF16)
        tab = _rope_tables(tab_ref[rows, :])
        z, hq, hg, dq, dk, dv, hi = (_dot(h, w[...]) for w in
                                     (whf_ref, whq_ref, whg_ref, wdq_ref, wdk_ref, wdv_ref, whi_ref))
        ez = jnp.exp(-jnp.abs(z))
        a1 = llb_ref[...]
        a2 = l1m_ref[...] + (jnp.minimum(z, 0.0) - jnp.log(1.0 + ez))
        hf_ref[rows, :] = jnp.maximum(a1, a2) + jnp.log(1.0 + jnp.exp(-jnp.abs(a1 - a2)))
        hk_ref[rows, :] = oml_ref[...] * (jnp.where(z >= 0.0, ez, 1.0) / (1.0 + ez))
        hq_ref[rows, :] = _silu(hq) * (HGRN_EXPAND ** -0.5)
        hg_ref[rows, :] = _silu(hg)
        dq_ref[rows, :] = (_rope(dq, tab) * (LOG2_E * HEAD_DIM ** -0.5)).astype(BF16)
        dk_ref[rows, :] = _rope(dk, tab).astype(BF16)
        for t in range(tiles):
            for hd in range(dvt_ref.shape[1]):
                tile = dv[t * TQ_DIFF:(t + 1) * TQ_DIFF, hd * LANES:(hd + 1) * LANES]
                dvt_ref[r * tiles + t, hd] = tile.T.astype(BF16)
        hi_ref[rows, :] = hi.astype(BF16)


def _tok_spec(width):
    return pl.BlockSpec((None, TM_IN, width), lambda b, t: (b, t, 0))


def _const_spec(shape):
    nd = len(shape)
    return pl.BlockSpec(shape, lambda b, t: (0,) * nd)


def _in_projection(kern, name, x, mod_l, nw, table, consts, weights, outs):
    bsz, seq, _ = x.shape
    in_specs = ([_tok_spec(D_MODEL), pl.BlockSpec((None, 1, 6 * D_MODEL), lambda b, t: (b, 0, 0)),
                 _const_spec(nw.shape), _tok_spec(2 * LANES)]
                + [_const_spec(a.shape) for a in consts]
                + [_const_spec(w.shape) for w in weights])
    out_specs = [_tok_spec(o[0]) if len(o) == 2 else o[1] for o in outs]
    out_shape = [jax.ShapeDtypeStruct((bsz, seq, o[0]) if len(o) == 2 else o[0], o[-1]) for o in outs]
    return pl.pallas_call(
        kern,
        grid=(bsz, seq // TM_IN),
        in_specs=in_specs,
        out_specs=out_specs,
        out_shape=out_shape,
        compiler_params=pltpu.CompilerParams(
            dimension_semantics=("arbitrary", "arbitrary"), vmem_limit_bytes=VMEM_LIMIT),
        name=name,
    )(x, mod_l, nw, table, *consts, *weights)


def _gl_kernel(q_ref, k_ref, g_ref, v_ref, gate_ref, nw_ref, o_ref, st_ref, cum_ref, inter_ref, *, heads):
    @pl.when(pl.program_id(2) == 0)
    def _():
        st_ref[...] = jnp.zeros_like(st_ref)

    dk = LANES // heads
    row = lax.broadcasted_iota(jnp.int32, (CHUNK, CHUNK), 0)
    col = lax.broadcasted_iota(jnp.int32, (CHUNK, CHUNK), 1)
    causal = row >= col
    tri = jnp.where(causal, 1.0, 0.0).astype(BF16)
    lane_sub = lax.broadcasted_iota(jnp.int32, (SUB, LANES), 1)
    lane_chunk = lax.broadcasted_iota(jnp.int32, (CHUNK, LANES), 1)
    nw = nw_ref[...]
    n_sub = CHUNK // SUB
    zeros_sub = jnp.zeros((SUB, LANES), BF16)

    def own(t, lane, p):
        return t if heads == 1 else jnp.where((lane >= p * dk) & (lane < (p + 1) * dk), t, 0.0)

    chunks = [slice(c * CHUNK, (c + 1) * CHUNK) for c in range(q_ref.shape[0] // CHUNK)]
    heads_r = range(heads)

    def sub_block_decay(g):
        return functools.reduce(jnp.maximum, [-jnp.sum(g[SUB * a:SUB * (a + 1)], axis=0, keepdims=True)
                                              for a in range(n_sub)])

    cum = []
    spread = jnp.zeros((1, LANES), F32)
    for rows in chunks:
        g = g_ref[rows, :]
        spread = jnp.maximum(spread, sub_block_decay(g))
        g1 = g.astype(BF16)
        r1 = g - g1.astype(F32)
        g2 = r1.astype(BF16)
        g3 = (r1 - g2.astype(F32)).astype(BF16)
        cum.append(_dot(tri, g1) + _dot(tri, g2) + _dot(tri, g3))

    q_in, decay_tot, k_own, k_stack, q_stack = [], [], [], [], []
    for rows, b in zip(chunks, cum):
        q = q_ref[rows, :]
        k = k_ref[rows, :]
        btot = b[CHUNK - 1:CHUNK, :]
        q_in.append((q * jnp.exp(b)).astype(BF16))
        k_out = k * jnp.exp(btot - b)
        k_own.append([own(k_out, lane_chunk, p).astype(BF16) for p in heads_r])
        decay_tot.append(jnp.exp(btot))
        k_groups, q_groups = [], []
        for a in range(n_sub):
            hi = SUB * (a + 1)
            lvl = b[SUB * a - 1:SUB * a, :] if a else jnp.zeros((1, LANES), F32)
            kk = (k[:hi] * jnp.exp(jnp.minimum(lvl - b[:hi], EXP_CLAMP))).astype(BF16)
            if hi < CHUNK:
                kk = jnp.concatenate([kk, jnp.zeros((CHUNK - hi, LANES), BF16)], axis=0)
            k_groups.append(kk)
            q_groups.append(q[SUB * a:hi] * jnp.exp(b[SUB * a:hi] - lvl))
        k_stack.append(jnp.concatenate(k_groups, axis=1))
        per_head = []
        for p in heads_r:
            q_rows = []
            for a in range(n_sub):
                qa = own(q_groups[a], lane_sub, p).astype(BF16)
                q_rows.append(jnp.concatenate([zeros_sub] * a + [qa] + [zeros_sub] * (n_sub - 1 - a), axis=1))
            per_head.append(jnp.concatenate(q_rows, axis=0))
        q_stack.append(per_head)

    vals = [[v_ref[rows, p * LANES:(p + 1) * LANES] for p in heads_r] for rows in chunks]
    scores = [[lax.dot_general(q_stack[c][p], k_stack[c], NT_DIMS, preferred_element_type=F32)
               for p in heads_r] for c in range(len(chunks))]
    incr = [[lax.dot_general(vals[c][p], k_own[c][p], TN_DIMS, preferred_element_type=F32)
             for p in heads_r] for c in range(len(chunks))]

    outs = []
    for p in heads_r:
        st = st_ref[p]
        for c in range(len(chunks)):
            s = jnp.where(causal, scores[c][p], 0.0).astype(BF16)
            inter = lax.dot_general(q_in[c], st.astype(BF16), NT_DIMS, preferred_element_type=F32)
            inter_ref[p, chunks[c], :] = inter
            o = _dot(s, vals[c][p]) + inter
            outs.append((c, p, o))
            st = st * decay_tot[c] + incr[c][p]
        st_ref[p] = st

    def finish(rows, p, o):
        cols = slice(p * LANES, (p + 1) * LANES)
        o_ref[rows, cols] = (_head_norm(o, nw) * gate_ref[rows, cols]).astype(BF16)

    for c, p, o in outs:
        finish(chunks[c], p, o)

    for c, b in enumerate(cum):
        cum_ref[chunks[c], :] = b

    @pl.when(jnp.max(spread) > EXP_CLAMP)
    def _():
        def redo_chunk(c, carry):
            r0 = pl.multiple_of(c * CHUNK, CHUNK)
            rows = pl.ds(r0, CHUNK)

            @pl.when(jnp.max(sub_block_decay(g_ref[rows, :])) > EXP_CLAMP)
            def _():
                q = q_ref[rows, :]
                b = cum_ref[rows, :]

                def key_column(j, acc):
                    kj = k_ref[pl.ds(r0 + j, 1), :]
                    bj = cum_ref[pl.ds(r0 + j, 1), :]
                    t = q * kj * jnp.exp(jnp.minimum(b - bj, 0.0))
                    return tuple(jnp.where(col == j, jnp.sum(own(t, lane_chunk, p), axis=1, keepdims=True),
                                           acc[p]) for p in heads_r)

                exact = lax.fori_loop(0, CHUNK, key_column,
                                      tuple(jnp.zeros((CHUNK, CHUNK), F32) for _ in heads_r))
                for p in heads_r:
                    s = jnp.where(causal, exact[p], 0.0).astype(BF16)
                    finish(rows, p, _dot(s, v_ref[rows, p * LANES:(p + 1) * LANES]) + inter_ref[p, rows, :])
            return carry

        lax.fori_loop(0, len(chunks), redo_chunk, 0)


def _gated_linear(q, k, g, v, gate, nw, heads):
    bsz, seq, width = q.shape
    groups = width // LANES
    vw = heads * LANES
    qspec = pl.BlockSpec((None, TS_GL, LANES), lambda b, gi, s: (b, s, gi))
    vspec = pl.BlockSpec((None, TS_GL, vw), lambda b, gi, s: (b, s, gi))
    return pl.pallas_call(
        functools.partial(_gl_kernel, heads=heads),
        grid=(bsz, groups, seq // TS_GL),
        in_specs=[qspec, qspec, qspec, vspec, vspec, pl.BlockSpec((1, LANES), lambda b, gi, s: (0, 0))],
        out_specs=vspec,
        out_shape=jax.ShapeDtypeStruct((bsz, seq, groups * vw), BF16),
        scratch_shapes=[pltpu.VMEM((heads, LANES, LANES), F32), pltpu.VMEM((TS_GL, LANES), F32),
                        pltpu.VMEM((heads, TS_GL, LANES), F32)],
        compiler_params=pltpu.CompilerParams(
            dimension_semantics=("arbitrary", "arbitrary", "arbitrary"), vmem_limit_bytes=VMEM_LIMIT),
        name=f"gated_linear_{heads}",
    )(q, k, g, v, gate, nw)


def _swa_kernel(sink_ref, q_ref, cur_ref, prev_ref, o_ref):
    n = pl.program_id(1)
    kj = lax.broadcasted_iota(jnp.int32, (2 * WINDOW, WINDOW), 0)
    qi = lax.broadcasted_iota(jnp.int32, (2 * WINDOW, WINDOW), 1) + WINDOW
    rel = qi - kj
    band = (rel >= 0) & (rel < WINDOW)
    band_first = band & ((n > 0) | (kj >= WINDOW))
    lane = lax.broadcasted_iota(jnp.int32, (WINDOW, LANES), 1)
    low = lane < HEAD_DIM
    per_kv = N_SWA_HEADS // N_SWA_KV_HEADS
    chains = [(t, i, kv) for t in range(q_ref.shape[0] // WINDOW)
              for i in range(per_kv) for kv in range(N_SWA_KV_HEADS)]
    values, scores = {}, {}
    for t in range(q_ref.shape[0] // WINDOW):
        rows = slice(t * WINDOW, (t + 1) * WINDOW)
        cur = cur_ref[rows, :]
        prev = cur_ref[(t - 1) * WINDOW:t * WINDOW, :] if t else prev_ref[...]
        keys = jnp.concatenate([prev[:, :LANES], cur[:, :LANES]], axis=0)
        vals = jnp.concatenate([prev[:, LANES:], cur[:, LANES:]], axis=0)
        vlow = lax.broadcasted_iota(jnp.int32, vals.shape, 1) < HEAD_DIM
        one = jnp.ones_like(vals)
        values[t] = (jnp.where(vlow, vals, one), jnp.where(vlow, one, vals))
        for i in range(per_kv):
            qg = q_ref[rows, i * LANES:(i + 1) * LANES]
            zero = jnp.zeros_like(qg)
            both = jnp.concatenate([jnp.where(low, qg, zero), jnp.where(low, zero, qg)], axis=0)
            s2 = lax.dot_general(keys, both, NT_DIMS, preferred_element_type=F32)
            scores[t, i, 0], scores[t, i, 1] = s2[:, :WINDOW], s2[:, WINDOW:]
    probs, sink_terms = {}, {}
    for t, i, kv in chains:
        sink = sink_ref[kv * per_kv + i] * LOG2_E
        s = jnp.where(band if t else band_first, scores[t, i, kv], -jnp.inf)
        m = jnp.maximum(jnp.max(s, axis=0, keepdims=True), sink)
        probs[t, i, kv] = jnp.exp2(s - m).astype(BF16)
        sink_terms[t, i, kv] = jnp.exp2(sink - m)
    outs = {}
    for t, i, kv in chains:
        if i % 2:
            continue
        pair = jnp.concatenate([probs[t, i, kv], probs[t, i + 1, kv]], axis=1)
        pv2 = lax.dot_general(values[t][kv], pair, TN_DIMS, preferred_element_type=F32)
        denom_row = HEAD_DIM if kv == 0 else 0
        for off, ii in ((0, i), (WINDOW, i + 1)):
            pv = pv2[:, off:off + WINDOW]
            outs[t, ii, kv] = pv * (1.0 / (pv[denom_row:denom_row + 1, :] + sink_terms[t, ii, kv]))
    top = lax.broadcasted_iota(jnp.int32, (LANES, WINDOW), 0) < HEAD_DIM
    for t, i, kv in chains:
        if kv == 0:
            o_t = jnp.where(top, outs[t, i, 0], outs[t, i, 1])
            o_ref[t * WINDOW:(t + 1) * WINDOW, i * LANES:(i + 1) * LANES] = o_t.T.astype(BF16)


def _sliding_window(sinks, q, kv):
    bsz, seq, _ = q.shape
    tq = SWA_BLOCKS * WINDOW
    return pl.pallas_call(
        _swa_kernel,
        grid=(bsz, seq // tq),
        in_specs=[
            pl.BlockSpec(memory_space=pltpu.SMEM),
            pl.BlockSpec((None, tq, MIX_HALF), lambda b, n: (b, n, 0)),
            pl.BlockSpec((None, tq, 2 * LANES), lambda b, n: (b, n, 0)),
            pl.BlockSpec((None, WINDOW, 2 * LANES), lambda b, n: (b, jnp.maximum(SWA_BLOCKS * n - 1, 0), 0)),
        ],
        out_specs=pl.BlockSpec((None, tq, MIX_HALF), lambda b, n: (b, n, 0)),
        out_shape=jax.ShapeDtypeStruct((bsz, seq, MIX_HALF), BF16),
        compiler_params=pltpu.CompilerParams(
            dimension_semantics=("arbitrary", "arbitrary"), vmem_limit_bytes=VMEM_LIMIT),
        name="sliding_window",
    )(sinks, q, kv, kv)


def _diff_kernel(lam_ref, q_ref, k_ref, vt_ref, nw_ref, o_ref, acc_ref, max_ref, s_ref, *, out_scale):
    qi = pl.program_id(1)
    tq = q_ref.shape[0]
    nh = q_ref.shape[1] // LANES
    lane = lax.broadcasted_iota(jnp.int32, (tq, LANES), 1)
    qs = []
    for h in range(nh):
        q = q_ref[:, h * LANES:(h + 1) * LANES]
        zero = jnp.zeros_like(q)
        qs += [jnp.where(lane < HEAD_DIM, q, zero), jnp.where(lane >= HEAD_DIM, q, zero)]
    key = lax.broadcasted_iota(jnp.int32, (tq, tq), 0)
    qry = lax.broadcasted_iota(jnp.int32, (tq, tq), 1)
    causal = key <= qry
    ones = jnp.ones((acc_ref.shape[1] - LANES, tq), BF16)
    acc_ref[...] = jnp.zeros_like(acc_ref)
    max_ref[...] = jnp.full(max_ref.shape, -jnp.inf, F32)

    def issue_scores(j, c):
        rows = pl.ds(pl.multiple_of(j * tq, tq), tq)
        kt = k_ref[rows, (c // 2) * LANES:(c // 2 + 1) * LANES]
        s_ref[c] = lax.dot_general(kt, qs[c], NT_DIMS, preferred_element_type=F32)

    def consume(j, c, masked):
        s = s_ref[c]
        if masked:
            s = jnp.where(causal, s, -jnp.inf)
        m = max_ref[c:c + 1, :]
        m_new = jnp.maximum(m, jnp.max(s, axis=0, keepdims=True))
        max_ref[c:c + 1, :] = m_new
        p = jnp.exp2(s - m_new).astype(BF16)
        vt1 = jnp.concatenate([vt_ref[j, c // 2], ones], axis=0)
        acc_ref[c] = jnp.exp2(m - m_new) * acc_ref[c] + _dot(vt1, p)

    for c in range(2 * nh):
        issue_scores(0, c)

    def body(j, carry):
        for c in range(2 * nh):
            consume(j, c, False)
            issue_scores(j + 1, c)
        return carry

    lax.fori_loop(0, qi, body, 0)
    for c in range(2 * nh):
        consume(qi, c, True)
    lam = lam_ref[0]
    nw = nw_ref[...]
    for h in range(nh):
        a1, a2 = acc_ref[2 * h], acc_ref[2 * h + 1]
        o_t = a1[:LANES] / a1[LANES:LANES + 1] - lam * (a2[:LANES] / a2[LANES:LANES + 1])
        o_ref[:, h * LANES:(h + 1) * LANES] = (_head_norm(o_t.T, nw) * out_scale).astype(BF16)


def _differential(lam, q, k, vt, nw, out_scale):
    bsz, seq, width = q.shape
    nh = width // LANES
    nt = seq // TQ_DIFF
    q_spec = pl.BlockSpec((None, TQ_DIFF, width), lambda b, i: (b, i, 0))
    return pl.pallas_call(
        functools.partial(_diff_kernel, out_scale=out_scale),
        grid=(bsz, nt),
        in_specs=[pl.BlockSpec(memory_space=pltpu.SMEM), q_spec,
                  pl.BlockSpec((None, seq, width), lambda b, i: (b, 0, 0)),
                  pl.BlockSpec((None, nt, nh, LANES, TQ_DIFF), lambda b, i: (b, 0, 0, 0, 0)),
                  pl.BlockSpec((1, LANES), lambda b, i: (0, 0))],
        out_specs=q_spec,
        out_shape=jax.ShapeDtypeStruct((bsz, seq, width), BF16),
        scratch_shapes=[pltpu.VMEM((2 * nh, LANES + ONES_ROWS, TQ_DIFF), F32),
                        pltpu.VMEM((2 * nh, TQ_DIFF), F32),
                        pltpu.VMEM((2 * nh, TQ_DIFF, TQ_DIFF), F32)],
        compiler_params=pltpu.CompilerParams(
            dimension_semantics=("arbitrary", "arbitrary"), vmem_limit_bytes=VMEM_LIMIT),
        name="differential",
    )(lam, q, k, vt, nw)


def _out_ffn_kernel(x_ref, oa_ref, ob_ref, mod_ref, nw_ref, fw_ref, woa_ref, wob_ref,
                    wi_ref, cw_ref, cb_ref, wo_ref,
                    y_ref, act_ref, carry_ref, *, final):
    t = pl.program_id(1)
    m = mod_ref[...]
    d = D_MODEL
    g1, sh2, sc2, g2 = m[:, 2 * d:3 * d], m[:, 3 * d:4 * d], m[:, 4 * d:5 * d], m[:, 5 * d:6 * d]
    @pl.when(t == 0)
    def _():
        carry_ref[...] = jnp.zeros_like(carry_ref)

    tm = act_ref.shape[0]
    row = lax.broadcasted_iota(jnp.int32, (tm, FF_CHUNK), 0)
    for r in range(x_ref.shape[0] // tm):
        rows = slice(r * tm, (r + 1) * tm)
        mix = _dot(oa_ref[rows, :], woa_ref[...]) + _dot(ob_ref[rows, :], wob_ref[...])
        x1 = x_ref[rows, :] + g1 * mix
        h = _modulated_norm(x1, nw_ref[...], sc2, sh2).astype(BF16)
        for c in range(D_FF // FF_CHUNK):
            cols = slice(c * FF_CHUNK, (c + 1) * FF_CHUNK)
            a = _dot(h, wi_ref[:, cols])
            u = _dot(h, wi_ref[:, D_FF + c * FF_CHUNK:D_FF + (c + 1) * FF_CHUNK])
            tail = carry_ref[:, cols]
            cw = cw_ref[:, cols]
            prev1 = jnp.where(row < 1, tail[7:8, :], pltpu.roll(a, 1, 0))
            prev2 = jnp.where(row < 1, tail[6:7, :], jnp.where(row < 2, tail[7:8, :], pltpu.roll(a, 2, 0)))
            carry_ref[:, cols] = a[tm - 8:tm, :]
            conv = prev2 * cw[0:1, :] + prev1 * cw[1:2, :] + a * cw[2:3, :] + cb_ref[:, cols]
            act_ref[:, cols] = (_silu(conv) * u).astype(BF16)
        y = x1 + g2 * _dot(act_ref[...], wo_ref[...])
        if final:
            ms = jnp.mean(y * y, axis=-1, keepdims=True)
            y = (y * lax.rsqrt(ms + NORM_EPS)) * fw_ref[...]
        y_ref[rows, :] = y


def _out_ffn(x, oa, ob, mod_l, nw, fw, woa, wob, wi_all, cw, cb, wo_all, layer, final):
    bsz, seq, _ = x.shape

    def resident(shape):
        nd = len(shape)
        return pl.BlockSpec(shape, lambda b, t: (0,) * nd, pipeline_mode=pl.Buffered(1))

    def layer_resident(shape):
        return pl.BlockSpec((None,) + shape[1:], lambda b, t: (layer, 0, 0), pipeline_mode=pl.Buffered(1))

    def tok(width):
        return pl.BlockSpec((None, TM_FFN, width), lambda b, t: (b, t, 0))

    return pl.pallas_call(
        functools.partial(_out_ffn_kernel, final=final),
        grid=(bsz, seq // TM_FFN),
        in_specs=[tok(D_MODEL), tok(MIX_HALF), tok(MIX_HALF),
                  pl.BlockSpec((None, 1, 6 * D_MODEL), lambda b, t: (b, 0, 0)),
                  _const_spec(nw.shape), _const_spec(fw.shape),
                  resident(woa.shape), resident(wob.shape), layer_resident(wi_all.shape),
                  _const_spec(cw.shape), _const_spec(cb.shape), layer_resident(wo_all.shape)],
        out_specs=tok(D_MODEL),
        out_shape=jax.ShapeDtypeStruct(x.shape, F32),
        scratch_shapes=[pltpu.VMEM((TM_PROJ, D_FF), BF16), pltpu.VMEM((8, D_FF), F32)],
        compiler_params=pltpu.CompilerParams(
            dimension_semantics=("arbitrary", "arbitrary"), vmem_limit_bytes=VMEM_LIMIT),
        name="out_ffn",
    )(x, oa, ob, mod_l, nw, fw, woa, wob, wi_all, cw, cb, wo_all)


def _rope_lane_table(positions):
    half = ROT_DIMS // 2
    inv_freq = ROPE_THETA ** (-jnp.arange(0, ROT_DIMS, 2, dtype=F32) / ROT_DIMS)
    ang = inv_freq[:, None] * positions.astype(F32).reshape(1, -1)
    cs = jnp.concatenate([jnp.cos(ang), jnp.sin(ang)], axis=0)
    d = np.arange(LANES) % HEAD_DIM
    sel = np.zeros((2 * half, 2 * LANES), np.float32)
    base = np.zeros((1, 2 * LANES), np.float32)
    for lane in range(LANES):
        if d[lane] < ROT_DIMS:
            sel[d[lane] % half, lane] = 1.0
            sel[half + d[lane] % half, LANES + lane] = -1.0 if d[lane] < half else 1.0
        else:
            base[0, lane] = 1.0
    c1 = cs.astype(BF16)
    r1 = cs - c1.astype(F32)
    c2 = r1.astype(BF16)
    c3 = (r1 - c2.astype(F32)).astype(BF16)
    sel_bf = jnp.asarray(sel, BF16)
    parts = [lax.dot_general(c, sel_bf, TN_DIMS, preferred_element_type=F32) for c in (c1, c2, c3)]
    tab = (parts[0] + parts[1]) + parts[2] + base
    return tab.reshape(positions.shape + (2 * LANES,))


def _cols(w, start, width):
    return w[:, start:start + width].astype(BF16)


SWA_HEAD_ORDER = (0, 4, 1, 5, 2, 6, 3, 7)


def kernel(x, c, positions, mod_w, mod_b, norm_mix_w, norm_ffn_w, ev_w_in, gla_gate_w, gla_gate_b,
           gla_norm_w, swa_sinks, ev_w_out, od_w_in, diff_lambda, diff_norm_w, hgrn_lb_logits,
           hgrn_norm_w, od_w_out, ffn_w_in, ffn_conv_w, ffn_conv_b, ffn_w_out, final_norm_w):
    bsz, seq, _ = x.shape
    table = _rope_lane_table(positions)
    lbs = jnp.cumsum(jax.nn.softmax(hgrn_lb_logits.astype(F32), axis=0), axis=0)
    lbs = lbs - lbs[0:1]
    mod = _modulation(c, mod_w, mod_b).reshape(DEPTH, bsz, 1, 6 * D_MODEL)
    swa_cols = jnp.concatenate([jnp.arange(HEAD_DIM) + HEAD_DIM * h for h in SWA_HEAD_ORDER])
    fw = final_norm_w.reshape(1, D_MODEL)
    ffn_wi, ffn_wo = ffn_w_in.astype(BF16), ffn_w_out.astype(BF16)

    for l in range(DEPTH):
        j = l // 2
        nw_mix = norm_mix_w[l].reshape(1, D_MODEL)
        if l % 2 == 0:
            w = ev_w_in[j]
            n_gla = GLA_QK + 2 * GLA_V
            sq_start = n_gla + GLA_GATE_RANK
            w_sq = w[:, sq_start:sq_start + SWA_Q][:, swa_cols].astype(BF16)
            w_glr = jnp.pad(w[:, n_gla:sq_start], ((0, 0), (0, LANES - GLA_GATE_RANK))).astype(BF16)
            gate_w = jnp.pad(gla_gate_w[j], ((0, LANES - GLA_GATE_RANK), (0, 0))).astype(BF16)
            weights = [_cols(w, 0, n_gla), w_glr, gate_w, gla_gate_b[j].reshape(1, -1), w_sq,
                       _cols(w, sq_start + SWA_Q, SWA_KV)]
            outs = [(GLA_QK // 2, F32), (GLA_QK // 2, F32), (GLA_V, BF16), (GLA_V, F32), (GLA_QK // 2, F32),
                    (SWA_Q, BF16), (SWA_KV, BF16)]
            gq, gk, gv, gr, ga, sq, skv = _in_projection(
                _even_in_kernel, "even_in", x, mod[l], nw_mix, table, [], weights, outs)
            oa = _gated_linear(gq, gk, ga, gv, gr, gla_norm_w[j].reshape(1, LANES), heads=2)
            ob = _sliding_window(swa_sinks[j], sq, skv)
            w_out = ev_w_out[j]
            wob = w_out[MIX_HALF:][swa_cols].astype(BF16)
        else:
            w = od_w_in[j]
            lb = lbs[j].reshape(1, -1)
            consts = [jnp.log(lb), jnp.log1p(-lb), 1.0 - lb]
            weights = [w.astype(BF16)]
            tiles = TM_IN // TQ_DIFF
            n_dh = MIX_HALF // LANES
            vt_out = ((bsz, seq // TQ_DIFF, n_dh, LANES, TQ_DIFF),
                      pl.BlockSpec((None, tiles, n_dh, LANES, TQ_DIFF), lambda b, t: (b, t, 0, 0, 0)), BF16)
            outs = [(512, BF16), (512, BF16), vt_out, (512, F32), (512, F32), (512, F32),
                    (512, BF16), (512, F32)]
            dq, dk, dvt, hq, hk, hf, hi, hg = _in_projection(
                _odd_in_kernel, "odd_in", x, mod[l], nw_mix, table, consts, weights, outs)
            lam_init = 0.8 - 0.6 * math.exp(-0.3 * l)
            lv = diff_lambda[j].astype(F32)
            lam = jnp.exp(jnp.sum(lv[0] * lv[1])) - jnp.exp(jnp.sum(lv[2] * lv[3])) + lam_init
            oa = _differential(lam.reshape(1), dq, dk, dvt, diff_norm_w[j].reshape(1, LANES), 1.0 - lam_init)
            ob = _gated_linear(hq, hk, hf, hi, hg, hgrn_norm_w[j].reshape(1, LANES), heads=1)
            w_out = od_w_out[j]
            wob = w_out[MIX_HALF:].astype(BF16)
        woa = w_out[:MIX_HALF].astype(BF16)
        x = _out_ffn(x, oa, ob, mod[l], norm_ffn_w[l].reshape(1, D_MODEL), fw, woa, wob,
                     ffn_wi, ffn_conv_w[l], ffn_conv_b[l].reshape(1, D_FF), ffn_wo,
                     layer=l, final=(l == DEPTH - 1))
    return x
```
